```python
import jax, jax.numpy as jnp
from jax import lax
import numpy as np

D_MODEL = 4096
BATCH = 4
SEQ = 2048
DEPTH = 2
DEC_BATCH = 128
DEC_SEQ = 4
PAST_LEN = 16384
PAGE_SIZE = 128

MIX_W = D_MODEL // 4
CHUNK = 128
A_HD = 128
A_HEADS = MIX_W // A_HD
B_BLOCKS = 8
B_BS = MIX_W // B_BLOCKS
CONV_W = 4
RG_C = 8.0
C_HEADS = 4
C_DK = MIX_W // 8
C_DV = MIX_W // C_HEADS
ROPE_BASE = 10000.0
D_HD = 64
D_HEADS = MIX_W // D_HD
D_DECAY_LORA = 64
D_AAA_LORA = 64
D_GATE_LORA = 160
D_COLS = 3 * MIX_W + D_DECAY_LORA + D_AAA_LORA + D_GATE_LORA
N_COLS = 6 * MIX_W + 2 * C_HEADS * C_DK + D_COLS
IN_SIZES = (MIX_W, MIX_W, MIX_W, MIX_W, C_HEADS * C_DK, C_HEADS * C_DK, MIX_W, MIX_W, D_COLS)
D_SIZES = (MIX_W, MIX_W, MIX_W, D_DECAY_LORA, D_AAA_LORA, D_GATE_LORA)
N_EXPERTS = 32
TOP_K = 4
D_FF = D_MODEL
SWIGLU_LIMIT = 7.0
SWIGLU_ALPHA = 1.702
MOE_BLOCK = 128
DEEPNORM_ALPHA = (2.0 * DEPTH) ** 0.25
DEEPNORM_BETA = (8.0 * DEPTH) ** -0.25
LN_EPS = 1e-5
RWKV_LN_EPS = 64e-5

kernel_name = 'hybrid_gmlp_rglru_retnet_rwkv7_moe_step'


def split_cols(t, sizes):
    return jnp.split(t, np.cumsum(sizes)[:-1].tolist(), axis=-1)


def layer_norm(x, eps=LN_EPS):
    xf = x.astype(jnp.float32)
    mu = jnp.mean(xf, axis=-1, keepdims=True)
    var = jnp.mean(jnp.square(xf - mu), axis=-1, keepdims=True)
    return ((xf - mu) * lax.rsqrt(var + eps)).astype(x.dtype)


def ada_modulation(c_act, w, b):
    mod = (c_act @ w + b)[:, None, :]
    shift, scale, gate = jnp.split(mod, 3, axis=-1)
    return shift, scale, 1.0 + gate


def rotary(x, pos):
    half = x.shape[-1] // 2
    inv = ROPE_BASE ** (-jnp.arange(half, dtype=jnp.float32) / half)
    ang = pos.astype(jnp.float32)[:, None] * inv[None, :]
    cos = jnp.cos(ang)[None, :, None, :]
    sin = jnp.sin(ang)[None, :, None, :]
    x1 = x[..., :half].astype(jnp.float32)
    x2 = x[..., half:].astype(jnp.float32)
    return jnp.concatenate([x1 * cos - x2 * sin, x1 * sin + x2 * cos], axis=-1)


def chunk_gating(u, v, ln_g, ln_b, ws, bs):
    Bq, T, _ = u.shape
    L = min(T, CHUNK)
    n = T // L
    vn = layer_norm(v) * ln_g + ln_b
    mask = jnp.tril(jnp.ones((L, L), dtype=bool))
    w = jnp.where(mask[None], ws[:, :L, :L], 0.0)
    vh = vn.reshape(Bq, n, L, A_HEADS, A_HD)
    mixed = jnp.einsum('hts,bnshd->bnthd', w, vh) + bs[:, :L].T[None, None, :, :, None]
    y = u * mixed.reshape(Bq, T, MIX_W)
    return y, vn[:, T - L:]


def rg_lru_branch(xb, gb, h0, conv_buf, conv_w, conv_b, w_r, b_r, w_i, b_i, lam):
    Bq, T, _ = xb.shape
    xpad = jnp.concatenate([conv_buf.astype(xb.dtype), xb], axis=1)
    xc = conv_b + sum(xpad[:, j:j + T] * conv_w[j] for j in range(CONV_W))
    xg = xc.reshape(Bq, T, B_BLOCKS, B_BS)
    r = jax.nn.sigmoid(jnp.einsum('btnc,ncd->btnd', xg, w_r).reshape(Bq, T, MIX_W) + b_r)
    i = jax.nn.sigmoid(jnp.einsum('btnc,ncd->btnd', xg, w_i).reshape(Bq, T, MIX_W) + b_i)
    log_a = (-RG_C * r * jax.nn.softplus(-lam)).astype(jnp.float32)
    a = jnp.exp(log_a)
    xin = jnp.sqrt(-jnp.expm1(2.0 * log_a)) * (i * xc).astype(jnp.float32)

    def step(h, inp):
        a_t, x_t = inp
        h = a_t * h + x_t
        return h, h

    hT, hs = lax.scan(step, h0.astype(jnp.float32), (jnp.moveaxis(a, 1, 0), jnp.moveaxis(xin, 1, 0)))
    y = jnp.moveaxis(hs, 0, 1).astype(xb.dtype) * jax.nn.gelu(gb, approximate=True)
    return y, hT.astype(h0.dtype), xpad[:, -(CONV_W - 1):].astype(conv_buf.dtype)


def retention_branch(q, k, v, g, pos, S0, chunk, gn_g, gn_b):
    Bq, T, _ = q.shape
    f32 = jnp.float32
    qh = rotary(q.reshape(Bq, T, C_HEADS, C_DK), pos)
    kh = rotary(k.reshape(Bq, T, C_HEADS, C_DK), pos) * (C_DK ** -0.5)
    vh = v.reshape(Bq, T, C_HEADS, C_DV).astype(f32)
    lg = jnp.log(1.0 - 2.0 ** (-5.0 - jnp.arange(C_HEADS, dtype=f32)))
    n = T // chunk
    idx = jnp.arange(chunk, dtype=f32)
    diff = idx[:, None] - idx[None, :]
    intra = jnp.where(diff[None] >= 0, jnp.exp(jnp.maximum(diff, 0.0)[None] * lg[:, None, None]), 0.0)
    q_decay = jnp.exp((idx[:, None] + 1.0) * lg[None, :])
    k_decay = jnp.exp((chunk - 1.0 - idx[:, None]) * lg[None, :])
    chunk_decay = jnp.exp(chunk * lg)

    def to_blocks(t):
        return jnp.moveaxis(t.reshape(Bq, n, chunk, C_HEADS, t.shape[-1]), 1, 0)

    def step(S, blk):
        qb, kb, vb = blk
        s = jnp.einsum('bihd,bjhd->bhij', qb, kb) * intra[None]
        y = jnp.einsum('bhij,bjhe->bihe', s, vb) + jnp.einsum('bihd,bhde->bihe', qb, S) * q_decay[None, :, :, None]
        S = S * chunk_decay[None, :, None, None] + jnp.einsum('bjhd,bjhe->bhde', kb * k_decay[None, :, :, None], vb)
        return S, y

    ST, ys = lax.scan(step, S0.astype(f32), (to_blocks(qh), to_blocks(kh), to_blocks(vh)))
    y = jnp.moveaxis(ys, 0, 1).reshape(Bq, T, C_HEADS, C_DV)
    y = layer_norm(y).reshape(Bq, T, MIX_W) * gn_g + gn_b
    return jax.nn.silu(g) * y.astype(g.dtype), ST.astype(S0.dtype)


def rwkv7_branch(f, shift_buf, S0, mu, w0, w2, a0, a2, g2, k_k, k_a, r_k, lnx_g, lnx_b):
    Bq, T, _ = f.shape
    f32 = jnp.float32
    prev = jnp.concatenate([shift_buf[:, None, :].astype(f.dtype), f[:, :-1]], axis=1)
    fs = f + mu * (prev - f)
    r, k, v, fw, fa, fg = split_cols(fs, D_SIZES)
    log_w = -jnp.exp(-jax.nn.softplus(-(w0 + jnp.tanh(fw) @ w2).astype(f32)) - 0.5)
    a = jax.nn.sigmoid((a0 + fa @ a2).astype(f32))
    g = jax.nn.sigmoid(fg) @ g2

    def heads(t):
        return t.astype(f32).reshape(Bq, T, D_HEADS, D_HD)

    kk = heads(k * k_k)
    kk = kk * lax.rsqrt(jnp.maximum(jnp.sum(kk * kk, axis=-1, keepdims=True), 1e-24))
    k_h = heads(k.astype(f32) * (1.0 + (a - 1.0) * k_a.astype(f32)))
    r_h, v_h, w_h, a_h = heads(r), heads(v), heads(log_w), heads(a)

    def step(S, inp):
        r_t, w_t, k_t, v_t, kk_t, a_t = inp
        sa = jnp.einsum('bhvk,bhk->bhv', S, kk_t)
        S = S * jnp.exp(w_t)[:, :, None, :] - sa[..., None] * (kk_t * a_t)[:, :, None, :] + v_t[..., None] * k_t[:, :, None, :]
        return S, jnp.einsum('bhvk,bhk->bhv', S, r_t)

    xs = tuple(jnp.moveaxis(t, 1, 0) for t in (r_h, w_h, k_h, v_h, kk, a_h))
    ST, ys = lax.scan(step, S0.astype(f32), xs)
    y = jnp.moveaxis(ys, 0, 1)
    y = layer_norm(y, RWKV_LN_EPS).reshape(Bq, T, MIX_W) * lnx_g + lnx_b
    bonus = jnp.sum(r_h * k_h * r_k.astype(f32), axis=-1, keepdims=True) * v_h
    y = y + bonus.reshape(Bq, T, MIX_W)
    return y.astype(f.dtype) * g, ST.astype(S0.dtype), f[:, -1].astype(shift_buf.dtype)


def moe_ffn(h, layer, w_router, b_router, w_up, b_up, w_down, b_down):
    Bq, T, D = h.shape
    n_tok = Bq * T
    tok = h.reshape(n_tok, D)
    logits = tok.astype(jnp.float32) @ w_router[layer].astype(jnp.float32) + b_router[layer].astype(jnp.float32)
    top_logit, top_idx = lax.top_k(logits, TOP_K)
    gates = jax.nn.softmax(top_logit, axis=-1).astype(h.dtype)
    n_assign = n_tok * TOP_K
    n_blocks = -(-n_assign // MOE_BLOCK) + N_EXPERTS
    flat_e = top_idx.reshape(-1).astype(jnp.int32)
    order = jnp.argsort(flat_e)
    sorted_e = flat_e[order]
    counts = jnp.zeros((N_EXPERTS,), jnp.int32).at[flat_e].add(1)
    blocks_e = (counts + MOE_BLOCK - 1) // MOE_BLOCK
    blk_end = jnp.cumsum(blocks_e)
    blk_start = blk_end - blocks_e
    assign_start = jnp.cumsum(counts) - counts
    rank = jnp.arange(n_assign, dtype=jnp.int32) - assign_start[sorted_e]
    dest = blk_start[sorted_e] * MOE_BLOCK + rank
    src_tok = order // TOP_K
    row_tok = jnp.zeros((n_blocks * MOE_BLOCK,), jnp.int32).at[dest].set(src_tok)
    blk_expert = jnp.minimum(jnp.searchsorted(blk_end, jnp.arange(n_blocks, dtype=jnp.int32), side='right'), N_EXPERTS - 1)
    xs = tok[row_tok].reshape(n_blocks, MOE_BLOCK, D)

    def expert_block(args):
        xb, e = args
        gu = xb @ w_up[layer, e] + b_up[layer, e]
        glu_in = jnp.minimum(gu[:, 0::2], SWIGLU_LIMIT)
        lin = jnp.clip(gu[:, 1::2], -SWIGLU_LIMIT, SWIGLU_LIMIT)
        act = glu_in * jax.nn.sigmoid(SWIGLU_ALPHA * glu_in) * (lin + 1.0)
        return act @ w_down[layer, e] + b_down[layer, e]

    ys = lax.map(expert_block, (xs, blk_expert)).reshape(n_blocks * MOE_BLOCK, D)
    w_sorted = gates.reshape(-1)[order]
    out = jnp.zeros_like(tok).at[src_tok].add(ys[dest] * w_sorted[:, None])
    return out.reshape(Bq, T, D)


def run_trunk(x, c, pos, ret_chunk, h_rg, conv_rg, s_ret, s_rwkv, shift_rwkv, p):
    c_act = jax.nn.silu(c)
    v_rows, h_out, conv_out, ret_out, rwkv_out, shift_out = [], [], [], [], [], []
    for l in range(DEPTH):
        shift, scale, gate = ada_modulation(c_act, p['w_ada'][l, 0], p['b_ada'][l, 0])
        h = layer_norm(x) * (1.0 + scale) + shift
        proj = h @ p['w_in'][l]
        a_u, a_v, b_x, b_g, c_q, c_k, c_v, c_g, d_f = split_cols(proj, IN_SIZES)
        y_a, vr = chunk_gating(a_u, a_v, p['a_ln_g'][l], p['a_ln_b'][l], p['a_ws'][l], p['a_bs'][l])
        y_b, hb, cb = rg_lru_branch(b_x, b_g, h_rg[:, l], conv_rg[:, l], p['b_conv_w'][l], p['b_conv_b'][l],
                                    p['b_wr'][l], p['b_br'][l], p['b_wi'][l], p['b_bi'][l], p['b_lambda'][l])
        y_c, sc = retention_branch(c_q, c_k, c_v, c_g, pos, s_ret[:, l], ret_chunk, p['c_gn_g'][l], p['c_gn_b'][l])
        y_d, sd, shd = rwkv7_branch(d_f, shift_rwkv[:, l], s_rwkv[:, l], p['d_mu'][l], p['d_w0'][l], p['d_w2'][l],
                                    p['d_a0'][l], p['d_a2'][l], p['d_g2'][l], p['d_kk'][l], p['d_ka'][l],
                                    p['d_rk'][l], p['d_lnx_g'][l], p['d_lnx_b'][l])
        mix = jnp.concatenate([y_a, y_b, y_c, y_d], axis=-1) @ p['w_out'][l]
        x = layer_norm(DEEPNORM_ALPHA * x + gate * mix) * p['ln_post_g'][l, 0] + p['ln_post_b'][l, 0]
        shift, scale, gate = ada_modulation(c_act, p['w_ada'][l, 1], p['b_ada'][l, 1])
        h = layer_norm(x) * (1.0 + scale) + shift
        ffn = moe_ffn(h, l, p['w_router'], p['b_router'], p['w_up'], p['b_up'], p['w_down'], p['b_down'])
        x = layer_norm(DEEPNORM_ALPHA * x + gate * ffn) * p['ln_post_g'][l, 1] + p['ln_post_b'][l, 1]
        v_rows.append(vr)
        h_out.append(hb)
        conv_out.append(cb)
        ret_out.append(sc)
        rwkv_out.append(sd)
        shift_out.append(shd)
    return (x, jnp.stack(v_rows, 1), jnp.stack(h_out, 1), jnp.stack(conv_out, 1),
            jnp.stack(ret_out, 1), jnp.stack(rwkv_out, 1), jnp.stack(shift_out, 1))


def setup_inputs(seed: int = 0) -> dict:
    key = jax.random.key(seed)
    ks = iter(jax.random.split(key, 64))

    def nrm(shape, scale):
        return scale * jax.random.normal(next(ks), shape, jnp.float32)

    def unif(shape, lo, hi):
        return jax.random.uniform(next(ks), shape, jnp.float32, lo, hi)

    a_init = unif((DEPTH, MIX_W), 0.9, 0.999)
    return {
        'x_prompt': nrm((BATCH, SEQ, D_MODEL), 1.0),
        'x_sample': nrm((DEC_BATCH, DEC_SEQ, D_MODEL), 1.0),
        'c_prompt': nrm((BATCH, D_MODEL), 1.0),
        'c_sample': nrm((DEC_BATCH, D_MODEL), 1.0),
        'state_rglru_h': nrm((DEC_BATCH, DEPTH, MIX_W), 0.5),
        'state_rglru_conv': nrm((DEC_BATCH, DEPTH, CONV_W - 1, MIX_W), 1.0),
        'state_retention': nrm((DEC_BATCH, DEPTH, C_HEADS, C_DK, C_DV), 0.5),
        'state_rwkv': nrm((DEC_BATCH, DEPTH, D_HEADS, D_HD, D_HD), 0.3),
        'state_rwkv_shift': nrm((DEC_BATCH, DEPTH, D_COLS), 1.0),
        'w_ada': nrm((DEPTH, 2, D_MODEL, 3 * D_MODEL), 0.1 * D_MODEL ** -0.5),
        'b_ada': nrm((DEPTH, 2, 3 * D_MODEL), 0.02),
        'ln_post_g': 1.0 + nrm((DEPTH, 2, D_MODEL), 0.02),
        'ln_post_b': nrm((DEPTH, 2, D_MODEL), 0.02),
        'w_in': nrm((DEPTH, D_MODEL, N_COLS), D_MODEL ** -0.5),
        'w_out': nrm((DEPTH, 4 * MIX_W, D_MODEL), DEEPNORM_BETA * (4 * MIX_W) ** -0.5),
        'a_ln_g': 1.0 + nrm((DEPTH, MIX_W), 0.02),
        'a_ln_b': nrm((DEPTH, MIX_W), 0.02),
        'a_ws': nrm((DEPTH, A_HEADS, CHUNK, CHUNK), 0.5 * CHUNK ** -0.5),
        'a_bs': 1.0 + nrm((DEPTH, A_HEADS, CHUNK), 0.02),
        'b_conv_w': nrm((DEPTH, CONV_W, MIX_W), 0.5),
        'b_conv_b': nrm((DEPTH, MIX_W), 0.02),
        'b_wr': nrm((DEPTH, B_BLOCKS, B_BS, B_BS), B_BS ** -0.5),
        'b_br': nrm((DEPTH, MIX_W), 0.02),
        'b_wi': nrm((DEPTH, B_BLOCKS, B_BS, B_BS), B_BS ** -0.5),
        'b_bi': nrm((DEPTH, MIX_W), 0.02),
        'b_lambda': jnp.log(a_init) - jnp.log1p(-a_init),
        'c_gn_g': 1.0 + nrm((DEPTH, MIX_W), 0.02),
        'c_gn_b': nrm((DEPTH, MIX_W), 0.02),
        'd_mu': unif((DEPTH, D_COLS), 0.0, 1.0),
        'd_w0': jnp.linspace(-6.5, -1.5, MIX_W, dtype=jnp.float32)[None, :] + nrm((DEPTH, MIX_W), 0.1),
        'd_w2': nrm((DEPTH, D_DECAY_LORA, MIX_W), 0.1),
        'd_a0': nrm((DEPTH, MIX_W), 0.1),
        'd_a2': nrm((DEPTH, D_AAA_LORA, MIX_W), 0.1),
        'd_g2': nrm((DEPTH, D_GATE_LORA, MIX_W), D_GATE_LORA ** -0.5),
        'd_kk': 0.85 + nrm((DEPTH, MIX_W), 0.02),
        'd_ka': 1.0 + nrm((DEPTH, MIX_W), 0.02),
        'd_rk': nrm((DEPTH, D_HEADS, D_HD), 0.1),
        'd_lnx_g': 1.0 + nrm((DEPTH, MIX_W), 0.02),
        'd_lnx_b': nrm((DEPTH, MIX_W), 0.02),
        'w_router': nrm((DEPTH, D_MODEL, N_EXPERTS), D_MODEL ** -0.5),
        'b_router': nrm((DEPTH, N_EXPERTS), 0.01),
        'w_up': nrm((DEPTH, N_EXPERTS, D_MODEL, 2 * D_FF), D_MODEL ** -0.5),
        'b_up': nrm((DEPTH, N_EXPERTS, 2 * D_FF), 0.02),
        'w_down': nrm((DEPTH, N_EXPERTS, D_FF, D_MODEL), DEEPNORM_BETA * D_FF ** -0.5),
        'b_down': nrm((DEPTH, N_EXPERTS, D_MODEL), 0.02),
    }


def reference(x_prompt, x_sample, c_prompt, c_sample, state_rglru_h, state_rglru_conv, state_retention,
              state_rwkv, state_rwkv_shift, w_ada, b_ada, ln_post_g, ln_post_b, w_in, w_out,
              a_ln_g, a_ln_b, a_ws, a_bs, b_conv_w, b_conv_b, b_wr, b_br, b_wi, b_bi, b_lambda,
              c_gn_g, c_gn_b, d_mu, d_w0, d_w2, d_a0, d_a2, d_g2, d_kk, d_ka, d_rk, d_lnx_g, d_lnx_b,
              w_router, b_router, w_up, b_up, w_down, b_down):
    p = dict(w_ada=w_ada, b_ada=b_ada, ln_post_g=ln_post_g, ln_post_b=ln_post_b, w_in=w_in, w_out=w_out,
             a_ln_g=a_ln_g, a_ln_b=a_ln_b, a_ws=a_ws, a_bs=a_bs, b_conv_w=b_conv_w, b_conv_b=b_conv_b,
             b_wr=b_wr, b_br=b_br, b_wi=b_wi, b_bi=b_bi, b_lambda=b_lambda, c_gn_g=c_gn_g, c_gn_b=c_gn_b,
             d_mu=d_mu, d_w0=d_w0, d_w2=d_w2, d_a0=d_a0, d_a2=d_a2, d_g2=d_g2, d_kk=d_kk, d_ka=d_ka,
             d_rk=d_rk, d_lnx_g=d_lnx_g, d_lnx_b=d_lnx_b, w_router=w_router, b_router=b_router,
             w_up=w_up, b_up=b_up, w_down=w_down, b_down=b_down)
    z_h = jnp.zeros((BATCH,) + state_rglru_h.shape[1:], state_rglru_h.dtype)
    z_conv = jnp.zeros((BATCH,) + state_rglru_conv.shape[1:], state_rglru_conv.dtype)
    z_ret = jnp.zeros((BATCH,) + state_retention.shape[1:], state_retention.dtype)
    z_rwkv = jnp.zeros((BATCH,) + state_rwkv.shape[1:], state_rwkv.dtype)
    z_shift = jnp.zeros((BATCH,) + state_rwkv_shift.shape[1:], state_rwkv_shift.dtype)
    pos_prompt = jnp.arange(SEQ, dtype=jnp.int32)
    pos_sample = PAST_LEN + jnp.arange(DEC_SEQ, dtype=jnp.int32)
    y_prompt, p_v, p_h, p_conv, p_ret, p_rwkv, p_shift = run_trunk(
        x_prompt, c_prompt, pos_prompt, min(CHUNK, SEQ), z_h, z_conv, z_ret, z_rwkv, z_shift, p)
    y_sample, s_v, s_h, s_conv, s_ret, s_rwkv, s_shift = run_trunk(
        x_sample, c_sample, pos_sample, DEC_SEQ, state_rglru_h, state_rglru_conv, state_retention,
        state_rwkv, state_rwkv_shift, p)
    return (y_prompt, y_sample, p_v, p_h, p_conv, p_ret, p_rwkv, p_shift, s_v, s_h, s_conv, s_ret, s_rwkv, s_shift)
```

```python
import functools

import jax
import jax.numpy as jnp
from jax import lax
import numpy as np
from jax.experimental import pallas as pl
from jax.experimental.pallas import tpu as pltpu

F32 = jnp.float32
BF16 = jnp.bfloat16

D_MODEL = 4096
DEPTH = 2
MIX_W = D_MODEL // 4
CHUNK = 128
A_HD = 128
A_HEADS = MIX_W // A_HD
B_BLOCKS = 8
B_BS = MIX_W // B_BLOCKS
CONV_W = 4
RG_C = 8.0
C_HEADS = 4
C_DK = MIX_W // 8
C_DV = MIX_W // C_HEADS
ROPE_BASE = 10000.0
PAST_LEN = 16384
D_HD = 64
D_HEADS = MIX_W // D_HD
D_DECAY_LORA = 64
D_AAA_LORA = 64
D_GATE_LORA = 160
D_COLS = 3 * MIX_W + D_DECAY_LORA + D_AAA_LORA + D_GATE_LORA
IN_SIZES = (MIX_W, MIX_W, MIX_W, MIX_W, C_HEADS * C_DK, C_HEADS * C_DK, MIX_W, MIX_W, D_COLS)
D_SIZES = (MIX_W, MIX_W, MIX_W, D_DECAY_LORA, D_AAA_LORA, D_GATE_LORA)
N_EXPERTS = 32
TOP_K = 4
D_FF = D_MODEL
SWIGLU_LIMIT = 7.0
SWIGLU_ALPHA = 1.702
DEEPNORM_ALPHA = (2.0 * DEPTH) ** 0.25
LN_EPS = 1e-5
RWKV_LN_EPS = 64e-5

VMEM_LIMIT_BYTES = 56 * 1024 * 1024

MM_TM = 512
MM_TN = 512
MOE_TM = 512
MOE_TF = 256


def _split_cols(t, sizes):
    return jnp.split(t, np.cumsum(sizes)[:-1].tolist(), axis=-1)


def _layer_norm(x, eps=LN_EPS):
    mu = jnp.mean(x, axis=-1, keepdims=True)
    var = jnp.mean(jnp.square(x - mu), axis=-1, keepdims=True)
    return (x - mu) * lax.rsqrt(var + eps)


def _mm_kernel(x_ref, w_ref, o_ref, wbf_ref):
    @pl.when(pl.program_id(1) == 0)
    def _():
        wbf_ref[...] = w_ref[...].astype(BF16)

    o_ref[...] = jnp.dot(x_ref[...], wbf_ref[...], preferred_element_type=F32)


def _mm(x, w, tm=MM_TM, tn=MM_TN):
    m, k = x.shape
    n = w.shape[1]
    tm = min(tm, m)
    return pl.pallas_call(
        _mm_kernel,
        grid=(pl.cdiv(n, tn), pl.cdiv(m, tm)),
        in_specs=[pl.BlockSpec((tm, k), lambda j, i: (i, 0)),
                  pl.BlockSpec((k, tn), lambda j, i: (0, j))],
        out_specs=pl.BlockSpec((tm, tn), lambda j, i: (i, j)),
        out_shape=jax.ShapeDtypeStruct((m, n), F32),
        scratch_shapes=[pltpu.VMEM((k, tn), BF16)],
        compiler_params=pltpu.CompilerParams(
            dimension_semantics=("arbitrary", "arbitrary"),
            vmem_limit_bytes=VMEM_LIMIT_BYTES),
    )(x, w)


def _router_kernel(x_ref, w_ref, b_ref, o_ref):
    o_ref[...] = jnp.dot(x_ref[...], w_ref[...], preferred_element_type=F32,
                         precision=lax.Precision.HIGHEST) + b_ref[...]


def _router(tok, w, b, tm=512):
    m, k = tok.shape
    n = w.shape[1]
    return pl.pallas_call(
        _router_kernel,
        grid=(m // tm,),
        in_specs=[pl.BlockSpec((tm, k), lambda i: (i, 0)),
                  pl.BlockSpec((k, n), lambda i: (0, 0)),
                  pl.BlockSpec((1, n), lambda i: (0, 0))],
        out_specs=pl.BlockSpec((tm, n), lambda i: (i, 0)),
        out_shape=jax.ShapeDtypeStruct((m, n), F32),
        compiler_params=pltpu.CompilerParams(
            dimension_semantics=("arbitrary",),
            vmem_limit_bytes=VMEM_LIMIT_BYTES),
    )(tok, w, b.reshape(1, n))


def _moe_kernel(be_ref, nv_ref, x_ref, wg_ref, wl_ref, bg_ref, bl_ref, wd_ref, bd_ref, o_ref):
    b = pl.program_id(0)
    f = pl.program_id(1)

    @pl.when(f == 0)
    def _():
        o_ref[...] = jnp.broadcast_to(bd_ref[0], o_ref.shape)

    @pl.when(nv_ref[b] > 0)
    def _():
        x = x_ref[...]
        g = jnp.dot(x, wg_ref[0], preferred_element_type=F32) + bg_ref[0]
        u = jnp.dot(x, wl_ref[0], preferred_element_type=F32) + bl_ref[0]
        g = jnp.minimum(g, SWIGLU_LIMIT)
        u = jnp.clip(u, -SWIGLU_LIMIT, SWIGLU_LIMIT)
        act = g * jax.nn.sigmoid(SWIGLU_ALPHA * g) * (u + 1.0)
        o_ref[...] += jnp.dot(act.astype(BF16), wd_ref[0], preferred_element_type=F32)


def _moe_grouped(xs, blk_expert, blk_valid, wg, wl, bg, bl, wd, bd):
    rows, d = xs.shape
    nb = rows // MOE_TM
    nf = D_FF // MOE_TF
    last_f = nf - 1

    def w_up_map(b, f, be, nv):
        return (be[b], 0, jnp.where(nv[b] > 0, f, last_f))

    def w_dn_map(b, f, be, nv):
        return (be[b], jnp.where(nv[b] > 0, f, last_f), 0)

    def b_up_map(b, f, be, nv):
        return (be[b], 0, jnp.where(nv[b] > 0, f, last_f))

    grid_spec = pltpu.PrefetchScalarGridSpec(
        num_scalar_prefetch=2,
        grid=(nb, nf),
        in_specs=[
            pl.BlockSpec((MOE_TM, d), lambda b, f, be, nv: (b, 0)),
            pl.BlockSpec((1, d, MOE_TF), w_up_map),
            pl.BlockSpec((1, d, MOE_TF), w_up_map),
            pl.BlockSpec((1, 1, MOE_TF), b_up_map),
            pl.BlockSpec((1, 1, MOE_TF), b_up_map),
            pl.BlockSpec((1, MOE_TF, d), w_dn_map),
            pl.BlockSpec((1, 1, d), lambda b, f, be, nv: (be[b], 0, 0)),
        ],
        out_specs=pl.BlockSpec((MOE_TM, d), lambda b, f, be, nv: (b, 0)),
    )
    return pl.pallas_call(
        _moe_kernel,
        grid_spec=grid_spec,
        out_shape=jax.ShapeDtypeStruct((rows, d), F32),
        compiler_params=pltpu.CompilerParams(
            dimension_semantics=("arbitrary", "arbitrary"),
            vmem_limit_bytes=VMEM_LIMIT_BYTES),
    )(blk_expert, blk_valid, xs, wg, wl, bg, bl, wd, bd)


def _moe_ffn(tok, w_router, b_router, wg, wl, bg, bl, wd, bd):
    n_tok = tok.shape[0]
    logits = _router(tok, w_router, b_router)
    top_logit, top_idx = lax.top_k(logits, TOP_K)
    gates = jax.nn.softmax(top_logit, axis=-1)
    n_assign = n_tok * TOP_K
    nb = -(-n_assign // MOE_TM) + N_EXPERTS
    flat_e = top_idx.reshape(-1).astype(jnp.int32)
    onehot = (flat_e[:, None] == jnp.arange(N_EXPERTS, dtype=jnp.int32)[None, :]).astype(jnp.int32)
    csum = jnp.cumsum(onehot, axis=0)
    counts = csum[-1]
    rank = jnp.take_along_axis(csum, flat_e[:, None], axis=1)[:, 0] - 1
    blocks_e = (counts + MOE_TM - 1) // MOE_TM
    blk_end = jnp.cumsum(blocks_e)
    blk_start = blk_end - blocks_e
    dest = blk_start[flat_e] * MOE_TM + rank
    src_tok = jnp.arange(n_assign, dtype=jnp.int32) // TOP_K
    row_tok = jnp.zeros((nb * MOE_TM,), jnp.int32).at[dest].set(src_tok)
    blk_ids = jnp.arange(nb, dtype=jnp.int32)
    blk_expert = jnp.minimum(jnp.searchsorted(blk_end, blk_ids, side='right'), N_EXPERTS - 1).astype(jnp.int32)
    blk_valid = (blk_ids < blk_end[-1]).astype(jnp.int32)
    xs = tok.astype(BF16)[row_tok]
    ys = _moe_grouped(xs, blk_expert, blk_valid, wg, wl, bg, bl, wd, bd)
    picked = ys[dest].reshape(n_tok, TOP_K, -1)
    return jnp.sum(picked * gates[:, :, None], axis=1)


def _rotary(x, pos):
    half = x.shape[-1] // 2
    inv = ROPE_BASE ** (-jnp.arange(half, dtype=F32) / half)
    ang = pos.astype(F32)[:, None] * inv[None, :]
    cos = jnp.cos(ang)[None, :, None, :]
    sin = jnp.sin(ang)[None, :, None, :]
    x1 = x[..., :half]
    x2 = x[..., half:]
    return jnp.concatenate([x1 * cos - x2 * sin, x1 * sin + x2 * cos], axis=-1)


def _chunk_gating(u, v, ln_g, ln_b, ws, bs):
    bq, t, _ = u.shape
    l = min(t, CHUNK)
    n = t // l
    vn = _layer_norm(v) * ln_g + ln_b
    mask = jnp.tril(jnp.ones((l, l), dtype=bool))
    w = jnp.where(mask[None], ws[:, :l, :l], 0.0)
    vh = vn.reshape(bq, n, l, A_HEADS, A_HD)
    mixed = jnp.einsum('hts,bnshd->bnthd', w, vh) + bs[:, :l].T[None, None, :, :, None]
    return u * mixed.reshape(bq, t, MIX_W), vn[:, t - l:]


def _rg_lru(xb, gb, h0, conv_buf, conv_w, conv_b, w_r, b_r, w_i, b_i, lam):
    bq, t, _ = xb.shape
    xpad = jnp.concatenate([conv_buf, xb], axis=1)
    xc = conv_b + sum(xpad[:, j:j + t] * conv_w[j] for j in range(CONV_W))
    xg = xc.reshape(bq, t, B_BLOCKS, B_BS)
    r = jax.nn.sigmoid(jnp.einsum('btnc,ncd->btnd', xg, w_r).reshape(bq, t, MIX_W) + b_r)
    i = jax.nn.sigmoid(jnp.einsum('btnc,ncd->btnd', xg, w_i).reshape(bq, t, MIX_W) + b_i)
    log_a = -RG_C * r * jax.nn.softplus(-lam)
    a = jnp.exp(log_a)
    xin = jnp.sqrt(-jnp.expm1(2.0 * log_a)) * (i * xc)

    def step(h, inp):
        a_t, x_t = inp
        h = a_t * h + x_t
        return h, h

    ht, hs = lax.scan(step, h0, (jnp.moveaxis(a, 1, 0), jnp.moveaxis(xin, 1, 0)))
    y = jnp.moveaxis(hs, 0, 1) * jax.nn.gelu(gb, approximate=True)
    return y, ht, xpad[:, -(CONV_W - 1):]


def _retention(q, k, v, g, pos, s0, chunk, gn_g, gn_b):
    bq, t, _ = q.shape
    qh = _rotary(q.reshape(bq, t, C_HEADS, C_DK), pos)
    kh = _rotary(k.reshape(bq, t, C_HEADS, C_DK), pos) * (C_DK ** -0.5)
    vh = v.reshape(bq, t, C_HEADS, C_DV)
    lg = jnp.log(1.0 - 2.0 ** (-5.0 - jnp.arange(C_HEADS, dtype=F32)))
    n = t // chunk
    idx = jnp.arange(chunk, dtype=F32)
    diff = idx[:, None] - idx[None, :]
    intra = jnp.where(diff[None] >= 0, jnp.exp(jnp.maximum(diff, 0.0)[None] * lg[:, None, None]), 0.0)
    q_decay = jnp.exp((idx[:, None] + 1.0) * lg[None, :])
    k_decay = jnp.exp((chunk - 1.0 - idx[:, None]) * lg[None, :])
    chunk_decay = jnp.exp(chunk * lg)

    def to_blocks(x):
        return jnp.moveaxis(x.reshape(bq, n, chunk, C_HEADS, x.shape[-1]), 1, 0)

    def step(s, blk):
        qb, kb, vb = blk
        sc = jnp.einsum('bihd,bjhd->bhij', qb, kb) * intra[None]
        y = jnp.einsum('bhij,bjhe->bihe', sc, vb) + jnp.einsum('bihd,bhde->bihe', qb, s) * q_decay[None, :, :, None]
        s = s * chunk_decay[None, :, None, None] + jnp.einsum('bjhd,bjhe->bhde', kb * k_decay[None, :, :, None], vb)
        return s, y

    st, ys = lax.scan(step, s0, (to_blocks(qh), to_blocks(kh), to_blocks(vh)))
    y = jnp.moveaxis(ys, 0, 1).reshape(bq, t, C_HEADS, C_DV)
    y = _layer_norm(y).reshape(bq, t, MIX_W) * gn_g + gn_b
    return jax.nn.silu(g) * y, st


def _rwkv7(f, shift_buf, s0, mu, w0, w2, a0, a2, g2, k_k, k_a, r_k, lnx_g, lnx_b):
    bq, t, _ = f.shape
    prev = jnp.concatenate([shift_buf[:, None, :], f[:, :-1]], axis=1)
    fs = f + mu * (prev - f)
    r, k, v, fw, fa, fg = _split_cols(fs, D_SIZES)
    log_w = -jnp.exp(-jax.nn.softplus(-(w0 + jnp.tanh(fw) @ w2)) - 0.5)
    a = jax.nn.sigmoid(a0 + fa @ a2)
    g = jax.nn.sigmoid(fg) @ g2

    def heads(x):
        return x.reshape(bq, t, D_HEADS, D_HD)

    kk = heads(k * k_k)
    kk = kk * lax.rsqrt(jnp.maximum(jnp.sum(kk * kk, axis=-1, keepdims=True), 1e-24))
    k_h = heads(k * (1.0 + (a - 1.0) * k_a))
    r_h, v_h, w_h, a_h = heads(r), heads(v), heads(log_w), heads(a)

    def step(s, inp):
        r_t, w_t, k_t, v_t, kk_t, a_t = inp
        sa = jnp.einsum('bhvk,bhk->bhv', s, kk_t)
        s = s * jnp.exp(w_t)[:, :, None, :] - sa[..., None] * (kk_t * a_t)[:, :, None, :] + v_t[..., None] * k_t[:, :, None, :]
        return s, jnp.einsum('bhvk,bhk->bhv', s, r_t)

    xs = tuple(jnp.moveaxis(x, 1, 0) for x in (r_h, w_h, k_h, v_h, kk, a_h))
    st, ys = lax.scan(step, s0, xs)
    y = jnp.moveaxis(ys, 0, 1)
    y = _layer_norm(y, RWKV_LN_EPS).reshape(bq, t, MIX_W) * lnx_g + lnx_b
    bonus = jnp.sum(r_h * k_h * r_k, axis=-1, keepdims=True) * v_h
    y = y + bonus.reshape(bq, t, MIX_W)
    return y * g, st, f[:, -1]


def _mixers(proj, l, pos, ret_chunk, h_rg, conv_rg, s_ret, s_rwkv, shift_rwkv, p):
    a_u, a_v, b_x, b_g, c_q, c_k, c_v, c_g, d_f = _split_cols(proj, IN_SIZES)
    y_a, vr = _chunk_gating(a_u, a_v, p['a_ln_g'][l], p['a_ln_b'][l], p['a_ws'][l], p['a_bs'][l])
    y_b, hb, cb = _rg_lru(b_x, b_g, h_rg, conv_rg, p['b_conv_w'][l], p['b_conv_b'][l],
                          p['b_wr'][l], p['b_br'][l], p['b_wi'][l], p['b_bi'][l], p['b_lambda'][l])
    y_c, sc = _retention(c_q, c_k, c_v, c_g, pos, s_ret, ret_chunk, p['c_gn_g'][l], p['c_gn_b'][l])
    y_d, sd, shd = _rwkv7(d_f, shift_rwkv, s_rwkv, p['d_mu'][l], p['d_w0'][l], p['d_w2'][l],
                          p['d_a0'][l], p['d_a2'][l], p['d_g2'][l], p['d_kk'][l], p['d_ka'][l],
                          p['d_rk'][l], p['d_lnx_g'][l], p['d_lnx_b'][l])
    return jnp.concatenate([y_a, y_b, y_c, y_d], axis=-1), (vr, hb, cb, sc, sd, shd)


def kernel(x_prompt, x_sample, c_prompt, c_sample, state_rglru_h, state_rglru_conv, state_retention, state_rwkv, state_rwkv_shift, w_ada, b_ada, ln_post_g, ln_post_b, w_in, w_out, a_ln_g, a_ln_b, a_ws, a_bs, b_conv_w, b_conv_b, b_wr, b_br, b_wi, b_bi, b_lambda, c_gn_g, c_gn_b, d_mu, d_w0, d_w2, d_a0, d_a2, d_g2, d_kk, d_ka, d_rk, d_lnx_g, d_lnx_b, w_router, b_router, w_up, b_up, w_down, b_down):
    p = dict(a_ln_g=a_ln_g, a_ln_b=a_ln_b, a_ws=a_ws, a_bs=a_bs, b_conv_w=b_conv_w, b_conv_b=b_conv_b,
             b_wr=b_wr, b_br=b_br, b_wi=b_wi, b_bi=b_bi, b_lambda=b_lambda, c_gn_g=c_gn_g, c_gn_b=c_gn_b,
             d_mu=d_mu, d_w0=d_w0, d_w2=d_w2, d_a0=d_a0, d_a2=d_a2, d_g2=d_g2, d_kk=d_kk, d_ka=d_ka,
             d_rk=d_rk, d_lnx_g=d_lnx_g, d_lnx_b=d_lnx_b)
    bp, tp, d = x_prompt.shape
    bs, ts, _ = x_sample.shape
    np_tok = bp * tp
    ns_tok = bs * ts

    c_act = jax.nn.silu(jnp.concatenate([c_prompt, c_sample], axis=0)).astype(BF16)

    def modulation(l, i):
        mod = _mm(c_act, w_ada[l, i]) + b_ada[l, i]
        shift, scale, gate = jnp.split(mod, 3, axis=-1)

        def rows(m):
            return jnp.concatenate([jnp.repeat(m[:bp], tp, axis=0), jnp.repeat(m[bp:], ts, axis=0)], axis=0)

        return rows(shift), rows(scale), rows(1.0 + gate)

    zeros = lambda s: jnp.zeros((bp,) + s.shape[1:], s.dtype)
    st_p = (zeros(state_rglru_h), zeros(state_rglru_conv), zeros(state_retention), zeros(state_rwkv),
            zeros(state_rwkv_shift))
    st_s = (state_rglru_h, state_rglru_conv, state_retention, state_rwkv, state_rwkv_shift)
    pos_p = jnp.arange(tp, dtype=jnp.int32)
    pos_s = PAST_LEN + jnp.arange(ts, dtype=jnp.int32)

    x = jnp.concatenate([x_prompt.reshape(np_tok, d), x_sample.reshape(ns_tok, d)], axis=0)
    outs_p, outs_s = [], []
    for l in range(DEPTH):
        shift, scale, gate = modulation(l, 0)
        h = (_layer_norm(x) * (1.0 + scale) + shift).astype(BF16)
        proj = _mm(h, w_in[l])
        mix_p, new_p = _mixers(proj[:np_tok].reshape(bp, tp, -1), l, pos_p, min(CHUNK, tp),
                               *(s[:, l] for s in st_p), p)
        mix_s, new_s = _mixers(proj[np_tok:].reshape(bs, ts, -1), l, pos_s, ts,
                               *(s[:, l] for s in st_s), p)
        outs_p.append(new_p)
        outs_s.append(new_s)
        mix_in = jnp.concatenate([mix_p.reshape(np_tok, d), mix_s.reshape(ns_tok, d)], axis=0).astype(BF16)
        mix = _mm(mix_in, w_out[l])
        x = _layer_norm(DEEPNORM_ALPHA * x + gate * mix) * ln_post_g[l, 0] + ln_post_b[l, 0]

        shift, scale, gate = modulation(l, 1)
        h = _layer_norm(x) * (1.0 + scale) + shift
        wg = w_up[l, :, :, 0::2].astype(BF16)
        wl = w_up[l, :, :, 1::2].astype(BF16)
        bg = b_up[l, :, 0::2].reshape(N_EXPERTS, 1, D_FF)
        bl = b_up[l, :, 1::2].reshape(N_EXPERTS, 1, D_FF)
        wd = w_down[l].astype(BF16)
        bd = b_down[l].reshape(N_EXPERTS, 1, d)
        ffn = _moe_ffn(h, w_router[l], b_router[l], wg, wl, bg, bl, wd, bd)
        x = _layer_norm(DEEPNORM_ALPHA * x + gate * ffn) * ln_post_g[l, 1] + ln_post_b[l, 1]

    def stack(outs, j):
        return jnp.stack([o[j] for o in outs], axis=1)

    y_prompt = x[:np_tok].reshape(bp, tp, d)
    y_sample = x[np_tok:].reshape(bs, ts, d)
    return (y_prompt, y_sample) + tuple(stack(outs_p, j) for j in range(6)) + tuple(stack(outs_s, j) for j in range(6))
```

```python
import functools

import jax
import jax.numpy as jnp
from jax import lax
import numpy as np
from jax.experimental import pallas as pl
from jax.experimental.pallas import tpu as pltpu

F32 = jnp.float32
BF16 = jnp.bfloat16

D_MODEL = 4096
DEPTH = 2
MIX_W = D_MODEL // 4
CHUNK = 128
A_HD = 128
A_HEADS = MIX_W // A_HD
B_BLOCKS = 8
B_BS = MIX_W // B_BLOCKS
CONV_W = 4
RG_C = 8.0
C_HEADS = 4
C_DK = MIX_W // 8
C_DV = MIX_W // C_HEADS
ROPE_BASE = 10000.0
PAST_LEN = 16384
D_HD = 64
D_HEADS = MIX_W // D_HD
D_DECAY_LORA = 64
D_AAA_LORA = 64
D_GATE_LORA = 160
D_COLS = 3 * MIX_W + D_DECAY_LORA + D_AAA_LORA + D_GATE_LORA
IN_SIZES = (MIX_W, MIX_W, MIX_W, MIX_W, C_HEADS * C_DK, C_HEADS * C_DK, MIX_W, MIX_W, D_COLS)
D_SIZES = (MIX_W, MIX_W, MIX_W, D_DECAY_LORA, D_AAA_LORA, D_GATE_LORA)
N_EXPERTS = 32
TOP_K = 4
D_FF = D_MODEL
SWIGLU_LIMIT = 7.0
SWIGLU_ALPHA = 1.702
DEEPNORM_ALPHA = (2.0 * DEPTH) ** 0.25
LN_EPS = 1e-5
RWKV_LN_EPS = 64e-5

VMEM_LIMIT_BYTES = 56 * 1024 * 1024

LANES = 128
SUBLANES = 8

MM_TM = 512
MM_TN = 512
MOE_TM = 256
MOE_TF = 512
MOE_TN = 1024
PREP_ROWS = 512
RWKV_TC = 64
RWKV_BB = 4
LRU_TC = 128


def _split_cols(t, sizes):
    return jnp.split(t, np.cumsum(sizes)[:-1].tolist(), axis=-1)


def _layer_norm(x, eps=LN_EPS):
    mu = jnp.mean(x, axis=-1, keepdims=True)
    var = jnp.mean(jnp.square(x - mu), axis=-1, keepdims=True)
    return (x - mu) * lax.rsqrt(var + eps)


def _mm_kernel(x_ref, w_ref, o_ref, wbf_ref):
    @pl.when(pl.program_id(1) == 0)
    def _():
        wbf_ref[...] = w_ref[...].astype(BF16)

    o_ref[...] = jnp.dot(x_ref[...], wbf_ref[...], preferred_element_type=F32)


def _mm(x, w, tm=MM_TM, tn=MM_TN):
    m, k = x.shape
    n = w.shape[1]
    tm = min(tm, m)
    return pl.pallas_call(
        _mm_kernel,
        grid=(pl.cdiv(n, tn), pl.cdiv(m, tm)),
        in_specs=[pl.BlockSpec((tm, k), lambda j, i: (i, 0)),
                  pl.BlockSpec((k, tn), lambda j, i: (0, j))],
        out_specs=pl.BlockSpec((tm, tn), lambda j, i: (i, j)),
        out_shape=jax.ShapeDtypeStruct((m, n), F32),
        scratch_shapes=[pltpu.VMEM((k, tn), BF16)],
        compiler_params=pltpu.CompilerParams(
            dimension_semantics=("arbitrary", "arbitrary"),
            vmem_limit_bytes=VMEM_LIMIT_BYTES),
    )(x, w)


def _router_kernel(x_ref, w_ref, b_ref, o_ref):
    o_ref[...] = jnp.dot(x_ref[...], w_ref[...], preferred_element_type=F32,
                         precision=lax.Precision.HIGHEST) + b_ref[...]


def _router(tok, w, b, tm=512):
    m, k = tok.shape
    n = w.shape[1]
    return pl.pallas_call(
        _router_kernel,
        grid=(m // tm,),
        in_specs=[pl.BlockSpec((tm, k), lambda i: (i, 0)),
                  pl.BlockSpec((k, n), lambda i: (0, 0)),
                  pl.BlockSpec((1, n), lambda i: (0, 0))],
        out_specs=pl.BlockSpec((tm, n), lambda i: (i, 0)),
        out_shape=jax.ShapeDtypeStruct((m, n), F32),
        compiler_params=pltpu.CompilerParams(
            dimension_semantics=("arbitrary",),
            vmem_limit_bytes=VMEM_LIMIT_BYTES),
    )(tok, w, b.reshape(1, n))


def _moe_up_kernel(be_ref, bs_ref, fi_ref, nv_ref, x_ref, w_ref, bg_ref, bl_ref, act_ref, wsc_ref):
    b = pl.program_id(1)
    d = w_ref.shape[2]
    n_groups = w_ref.shape[3] // (2 * LANES)

    @pl.when(fi_ref[b] == 1)
    def _():
        ii = lax.broadcasted_iota(jnp.int32, (2 * LANES, 2 * LANES), 0)
        jj = lax.broadcasted_iota(jnp.int32, (2 * LANES, 2 * LANES), 1)
        src = jnp.where(jj < LANES, 2 * jj, 2 * (jj - LANES) + 1)
        perm = jnp.where(ii == src, 1.0, 0.0).astype(BF16)

        def body(r, carry):
            rows = pl.ds(pl.multiple_of(r * PREP_ROWS, PREP_ROWS), PREP_ROWS)
            for c in range(n_groups):
                cols = slice(c * 2 * LANES, (c + 1) * 2 * LANES)
                wb = w_ref[0, 0, rows, cols].astype(BF16)
                wsc_ref[rows, cols] = jnp.dot(wb, perm, preferred_element_type=F32).astype(BF16)
            return carry

        lax.fori_loop(0, d // PREP_ROWS, body, 0)

    @pl.when(nv_ref[b] > 0)
    def _():
        gu = jnp.dot(x_ref[...], wsc_ref[...], preferred_element_type=F32)
        for c in range(n_groups):
            g = gu[:, c * 2 * LANES:c * 2 * LANES + LANES] + bg_ref[0, 0, :, c * LANES:(c + 1) * LANES]
            u = gu[:, c * 2 * LANES + LANES:(c + 1) * 2 * LANES] + bl_ref[0, 0, :, c * LANES:(c + 1) * LANES]
            g = jnp.minimum(g, SWIGLU_LIMIT)
            u = jnp.clip(u, -SWIGLU_LIMIT, SWIGLU_LIMIT)
            act = g * jax.nn.sigmoid(SWIGLU_ALPHA * g) * (u + 1.0)
            act_ref[:, c * LANES:(c + 1) * LANES] = act.astype(BF16)

    @pl.when(nv_ref[b] == 0)
    def _():
        act_ref[...] = jnp.zeros(act_ref.shape, act_ref.dtype)


def _moe_down_kernel(be_ref, bs_ref, fi_ref, nv_ref, a_ref, w_ref, bd_ref, o_ref, wsc_ref):
    b = pl.program_id(1)
    f = w_ref.shape[2]

    @pl.when(fi_ref[b] == 1)
    def _():
        def body(r, carry):
            rows = pl.ds(pl.multiple_of(r * PREP_ROWS, PREP_ROWS), PREP_ROWS)
            wsc_ref[rows, :] = w_ref[0, 0, rows, :].astype(BF16)
            return carry

        lax.fori_loop(0, f // PREP_ROWS, body, 0)

    @pl.when(nv_ref[b] > 0)
    def _():
        o_ref[...] = jnp.dot(a_ref[...], wsc_ref[...], preferred_element_type=F32) + bd_ref[0, 0]

    @pl.when(nv_ref[b] == 0)
    def _():
        o_ref[...] = jnp.zeros(o_ref.shape, o_ref.dtype)


def _moe_experts(xs, sched, w_up, bg, bl, w_down, b_down, layer, tm, tf, tn):
    rows, d = xs.shape
    nb = rows // tm
    f_dim = w_down.shape[2]
    params = pltpu.CompilerParams(dimension_semantics=("arbitrary", "arbitrary"),
                                  vmem_limit_bytes=VMEM_LIMIT_BYTES)
    up_spec = pltpu.PrefetchScalarGridSpec(
        num_scalar_prefetch=4,
        grid=(f_dim // tf, nb),
        in_specs=[
            pl.BlockSpec((tm, d), lambda f, b, be, bs, fi, nv: (bs[b], 0)),
            pl.BlockSpec((1, 1, d, 2 * tf), lambda f, b, be, bs, fi, nv: (layer, be[b], 0, f)),
            pl.BlockSpec((1, 1, 1, tf), lambda f, b, be, bs, fi, nv: (layer, be[b], 0, f)),
            pl.BlockSpec((1, 1, 1, tf), lambda f, b, be, bs, fi, nv: (layer, be[b], 0, f)),
        ],
        out_specs=pl.BlockSpec((tm, tf), lambda f, b, be, bs, fi, nv: (b, f)),
        scratch_shapes=[pltpu.VMEM((d, 2 * tf), BF16)],
    )
    act = pl.pallas_call(
        _moe_up_kernel, grid_spec=up_spec,
        out_shape=jax.ShapeDtypeStruct((rows, f_dim), BF16),
        compiler_params=params, name="moe_up",
    )(*sched, xs, w_up, bg, bl)
    down_spec = pltpu.PrefetchScalarGridSpec(
        num_scalar_prefetch=4,
        grid=(d // tn, nb),
        in_specs=[
            pl.BlockSpec((tm, f_dim), lambda n, b, be, bs, fi, nv: (bs[b], 0)),
            pl.BlockSpec((1, 1, f_dim, tn), lambda n, b, be, bs, fi, nv: (layer, be[b], 0, n)),
            pl.BlockSpec((1, 1, 1, tn), lambda n, b, be, bs, fi, nv: (layer, be[b], 0, n)),
        ],
        out_specs=pl.BlockSpec((tm, tn), lambda n, b, be, bs, fi, nv: (b, n)),
        scratch_shapes=[pltpu.VMEM((f_dim, tn), BF16)],
    )
    return pl.pallas_call(
        _moe_down_kernel, grid_spec=down_spec,
        out_shape=jax.ShapeDtypeStruct((rows, d), F32),
        compiler_params=params, name="moe_down",
    )(*sched, act, w_down, b_down)


def _moe_ffn(tok, w_router, b_router, w_up, bg, bl, w_down, b_down, layer,
             tm=MOE_TM, tf=MOE_TF, tn=MOE_TN):
    n_tok = tok.shape[0]
    n_exp = w_router.shape[1]
    logits = _router(tok, w_router, b_router)
    top_logit, top_idx = lax.top_k(logits, TOP_K)
    gates = jax.nn.softmax(top_logit, axis=-1)
    n_assign = n_tok * TOP_K
    nb = -(-n_assign // tm) + n_exp
    flat_e = top_idx.reshape(-1).astype(jnp.int32)
    onehot = (flat_e[:, None] == jnp.arange(n_exp, dtype=jnp.int32)[None, :]).astype(jnp.int32)
    csum = jnp.cumsum(onehot, axis=0)
    counts = csum[-1]
    rank = jnp.take_along_axis(csum, flat_e[:, None], axis=1)[:, 0] - 1
    blocks_e = (counts + tm - 1) // tm
    blk_end = jnp.cumsum(blocks_e)
    blk_start = blk_end - blocks_e
    dest = blk_start[flat_e] * tm + rank
    src_tok = jnp.arange(n_assign, dtype=jnp.int32) // TOP_K
    row_tok = jnp.zeros((nb * tm,), jnp.int32).at[dest].set(src_tok)
    blk_ids = jnp.arange(nb, dtype=jnp.int32)
    blk_src = jnp.minimum(blk_ids, blk_end[-1] - 1).astype(jnp.int32)
    blk_expert = jnp.minimum(jnp.searchsorted(blk_end, blk_src, side='right'), n_exp - 1).astype(jnp.int32)
    blk_valid = (blk_ids < blk_end[-1]).astype(jnp.int32)
    blk_first = jnp.concatenate([jnp.ones((1,), jnp.int32),
                                 (blk_expert[1:] != blk_expert[:-1]).astype(jnp.int32)])
    sched = (blk_expert, blk_src, blk_first, blk_valid)
    xs = tok.astype(BF16)[row_tok]
    ys = _moe_experts(xs, sched, w_up, bg, bl, w_down, b_down, layer, tm, tf, tn)
    picked = ys[dest].reshape(n_tok, TOP_K, -1)
    return jnp.sum(picked * gates[:, :, None], axis=1)


def _segsum(p, ones_blk, split):
    hi = p.astype(BF16)
    parts = [hi]
    if split:
        parts.append((p - hi.astype(F32)).astype(BF16))
    n_seg = p.shape[1] // (2 * LANES)
    stacked = jnp.concatenate([x[:, j * 2 * LANES:(j + 1) * 2 * LANES] for x in parts for j in range(n_seg)], axis=0)
    r = jnp.dot(stacked, ones_blk, preferred_element_type=F32)
    rows = p.shape[0] * n_seg
    out = r[:rows]
    if split:
        out = out + r[rows:]
    return jnp.concatenate([out[j * p.shape[0]:(j + 1) * p.shape[0]] for j in range(n_seg)], axis=1)


def _rwkv_kernel(kk_ref, z_ref, w_ref, b_ref, k_ref, vt_ref, s0_ref, y_ref, st_ref, s_scr):
    nb, tc, width = kk_ref.shape
    hd = s_scr.shape[1]

    @pl.when(pl.program_id(1) == 0)
    def _():
        s_scr[...] = s0_ref[...]

    ri = lax.broadcasted_iota(jnp.int32, (2 * LANES, 2 * LANES), 0)
    ci = lax.broadcasted_iota(jnp.int32, (2 * LANES, 2 * LANES), 1)
    ones_blk = jnp.where((ri // hd) == (ci // hd), 1.0, 0.0).astype(BF16)
    er = lax.broadcasted_iota(jnp.int32, (LANES, width), 0)
    ec = lax.broadcasted_iota(jnp.int32, (LANES, width), 1)
    n_heads = width // hd
    expand = jnp.where(jnp.where(er < 3 * n_heads, er % n_heads, -1) == (ec // hd), 1.0, 0.0).astype(BF16)
    dr = lax.broadcasted_iota(jnp.int32, (hd, width), 0)
    dc = lax.broadcasted_iota(jnp.int32, (hd, width), 1)
    diag = (dc % hd) == dr

    def step(t, carry):
        for b in range(nb):
            s = s_scr[b]
            sa = _segsum(s * kk_ref[b, pl.ds(t, 1), :], ones_blk, True)
            q = _segsum(s * z_ref[b, pl.ds(t, 1), :], ones_blk, False)
            ve = jnp.dot(vt_ref[b, t], expand, preferred_element_type=F32)
            s_scr[b] = s * w_ref[b, pl.ds(t, 1), :] - sa * b_ref[b, pl.ds(t, 1), :] + ve * k_ref[b, pl.ds(t, 1), :]
            yd = jnp.where(diag, q, 0.0)
            y_ref[b, t] = jnp.sum(yd.reshape(hd // SUBLANES, SUBLANES, width), axis=0)
        return carry

    lax.fori_loop(0, tc, step, 0)
    st_ref[...] = s_scr[...]


def _rwkv_scan(kk, z, w, bb, k, vt, s0, tc):
    nb, t, width = kk.shape
    hd = s0.shape[1]
    gb = RWKV_BB
    row = pl.BlockSpec((gb, tc, width), lambda bi, ti: (bi, ti, 0))
    state = pl.BlockSpec((gb, hd, width), lambda bi, ti: (bi, 0, 0))
    return pl.pallas_call(
        _rwkv_kernel,
        grid=(nb // gb, t // tc),
        in_specs=[row, row, row, row, row,
                  pl.BlockSpec((gb, tc, hd, LANES), lambda bi, ti: (bi, ti, 0, 0)),
                  state],
        out_specs=[pl.BlockSpec((gb, tc, SUBLANES, width), lambda bi, ti: (bi, ti, 0, 0)),
                   state],
        out_shape=[jax.ShapeDtypeStruct((nb, t, SUBLANES, width), F32),
                   jax.ShapeDtypeStruct((nb, hd, width), F32)],
        scratch_shapes=[pltpu.VMEM((gb, hd, width), F32)],
        compiler_params=pltpu.CompilerParams(dimension_semantics=("arbitrary", "arbitrary"),
                                             vmem_limit_bytes=VMEM_LIMIT_BYTES),
        name="rwkv_scan",
    )(kk, z, w, bb, k, vt, s0)


def _lru_kernel(a_ref, x_ref, h0_ref, hs_ref, h_scr):
    @pl.when(pl.program_id(0) == 0)
    def _():
        h_scr[...] = h0_ref[...]

    def step(t, h):
        h = a_ref[t] * h + x_ref[t]
        hs_ref[t] = h
        return h

    h_scr[...] = lax.fori_loop(0, a_ref.shape[0], step, h_scr[...], unroll=8)


def _lru_scan(a, x, h0, tc=LRU_TC):
    t, nb, width = a.shape
    blk = pl.BlockSpec((tc, nb, width), lambda i: (i, 0, 0))
    return pl.pallas_call(
        _lru_kernel,
        grid=(t // tc,),
        in_specs=[blk, blk, pl.BlockSpec((nb, width), lambda i: (0, 0))],
        out_specs=blk,
        out_shape=jax.ShapeDtypeStruct((t, nb, width), F32),
        scratch_shapes=[pltpu.VMEM((nb, width), F32)],
        compiler_params=pltpu.CompilerParams(dimension_semantics=("arbitrary",),
                                             vmem_limit_bytes=VMEM_LIMIT_BYTES),
        name="lru_scan",
    )(a, x, h0)


def _rotary(x, pos):
    half = x.shape[-1] // 2
    inv = ROPE_BASE ** (-jnp.arange(half, dtype=F32) / half)
    ang = pos.astype(F32)[:, None] * inv[None, :]
    cos = jnp.cos(ang)[None, :, None, :]
    sin = jnp.sin(ang)[None, :, None, :]
    x1 = x[..., :half]
    x2 = x[..., half:]
    return jnp.concatenate([x1 * cos - x2 * sin, x1 * sin + x2 * cos], axis=-1)


def _chunk_gating(u, v, ln_g, ln_b, ws, bs):
    bq, t, _ = u.shape
    l = min(t, CHUNK)
    n = t // l
    vn = _layer_norm(v) * ln_g + ln_b
    mask = jnp.tril(jnp.ones((l, l), dtype=bool))
    w = jnp.where(mask[None], ws[:, :l, :l], 0.0)
    vh = vn.reshape(bq, n, l, A_HEADS, A_HD)
    mixed = jnp.einsum('hts,bnshd->bnthd', w, vh) + bs[:, :l].T[None, None, :, :, None]
    return u * mixed.reshape(bq, t, MIX_W), vn[:, t - l:]


def _rg_lru(xb, gb, h0, conv_buf, conv_w, conv_b, w_r, b_r, w_i, b_i, lam):
    bq, t, _ = xb.shape
    xpad = jnp.concatenate([conv_buf, xb], axis=1)
    xc = conv_b + sum(xpad[:, j:j + t] * conv_w[j] for j in range(CONV_W))
    xg = xc.reshape(bq, t, B_BLOCKS, B_BS)
    r = jax.nn.sigmoid(jnp.einsum('btnc,ncd->btnd', xg, w_r).reshape(bq, t, MIX_W) + b_r)
    i = jax.nn.sigmoid(jnp.einsum('btnc,ncd->btnd', xg, w_i).reshape(bq, t, MIX_W) + b_i)
    log_a = -RG_C * r * jax.nn.softplus(-lam)
    a = jnp.exp(log_a)
    xin = jnp.sqrt(-jnp.expm1(2.0 * log_a)) * (i * xc)

    hs = jnp.moveaxis(_lru_scan(jnp.moveaxis(a, 1, 0), jnp.moveaxis(xin, 1, 0), h0, min(LRU_TC, t)), 0, 1)
    y = hs * jax.nn.gelu(gb, approximate=True)
    return y, hs[:, -1], xpad[:, -(CONV_W - 1):]


def _retention(q, k, v, g, pos, s0, chunk, gn_g, gn_b):
    bq, t, _ = q.shape
    qh = _rotary(q.reshape(bq, t, C_HEADS, C_DK), pos)
    kh = _rotary(k.reshape(bq, t, C_HEADS, C_DK), pos) * (C_DK ** -0.5)
    vh = v.reshape(bq, t, C_HEADS, C_DV)
    lg = jnp.log(1.0 - 2.0 ** (-5.0 - jnp.arange(C_HEADS, dtype=F32)))
    n = t // chunk
    idx = jnp.arange(chunk, dtype=F32)
    diff = idx[:, None] - idx[None, :]
    intra = jnp.where(diff[None] >= 0, jnp.exp(jnp.maximum(diff, 0.0)[None] * lg[:, None, None]), 0.0)
    q_decay = jnp.exp((idx[:, None] + 1.0) * lg[None, :])
    k_decay = jnp.exp((chunk - 1.0 - idx[:, None]) * lg[None, :])
    chunk_decay = jnp.exp(chunk * lg)

    def to_blocks(x):
        return jnp.moveaxis(x.reshape(bq, n, chunk, C_HEADS, x.shape[-1]), 1, 0)

    def step(s, blk):
        qb, kb, vb = blk
        sc = jnp.einsum('bihd,bjhd->bhij', qb, kb) * intra[None]
        y = jnp.einsum('bhij,bjhe->bihe', sc, vb) + jnp.einsum('bihd,bhde->bihe', qb, s) * q_decay[None, :, :, None]
        s = s * chunk_decay[None, :, None, None] + jnp.einsum('bjhd,bjhe->bhde', kb * k_decay[None, :, :, None], vb)
        return s, y

    st, ys = lax.scan(step, s0, (to_blocks(qh), to_blocks(kh), to_blocks(vh)))
    y = jnp.moveaxis(ys, 0, 1).reshape(bq, t, C_HEADS, C_DV)
    y = _layer_norm(y).reshape(bq, t, MIX_W) * gn_g + gn_b
    return jax.nn.silu(g) * y, st


def _split3(x):
    hi = x.astype(BF16)
    r1 = x - hi.astype(F32)
    mid = r1.astype(BF16)
    lo = (r1 - mid.astype(F32)).astype(BF16)
    return hi, mid, lo


def _rwkv7(f, shift_buf, s0, mu, w0, w2, a0, a2, g2, k_k, k_a, r_k, lnx_g, lnx_b):
    bq, t, _ = f.shape
    prev = jnp.concatenate([shift_buf[:, None, :], f[:, :-1]], axis=1)
    fs = f + mu * (prev - f)
    r, k, v, fw, fa, fg = _split_cols(fs, D_SIZES)
    log_w = -jnp.exp(-jax.nn.softplus(-(w0 + jnp.tanh(fw) @ w2)) - 0.5)
    a = jax.nn.sigmoid(a0 + fa @ a2)
    g = jax.nn.sigmoid(fg) @ g2

    def heads(x):
        return x.reshape(bq, t, D_HEADS, D_HD)

    kk = heads(k * k_k)
    kk = kk * lax.rsqrt(jnp.maximum(jnp.sum(kk * kk, axis=-1, keepdims=True), 1e-24))
    k_h = heads(k * (1.0 + (a - 1.0) * k_a))
    r_h, v_h, w_h, a_h = heads(r), heads(v), heads(log_w), heads(a)

    def flat(x):
        return x.reshape(bq, t, MIX_W)

    def head_sum(x):
        return jnp.broadcast_to(jnp.sum(x, axis=-1, keepdims=True), x.shape)

    decay = jnp.exp(w_h)
    b_h = kk * a_h
    z_h = decay * r_h - kk * head_sum(b_h * r_h)
    vt = jnp.concatenate(_split3(jnp.swapaxes(v_h, 2, 3)), axis=-1)
    vt = jnp.pad(vt, ((0, 0), (0, 0), (0, 0), (0, LANES - vt.shape[-1])))
    s0k = jnp.swapaxes(s0, 1, 2).reshape(bq, D_HD, MIX_W)
    y8, st = _rwkv_scan(flat(kk), flat(z_h), flat(decay), flat(b_h), flat(k_h), vt, s0k, min(RWKV_TC, t))
    y = jnp.sum(y8, axis=2).reshape(bq, t, D_HEADS, D_HD) + head_sum(k_h * r_h) * v_h
    st = jnp.swapaxes(st.reshape(bq, D_HD, D_HEADS, D_HD), 1, 2)
    y = _layer_norm(y, RWKV_LN_EPS).reshape(bq, t, MIX_W) * lnx_g + lnx_b
    bonus = jnp.sum(r_h * k_h * r_k, axis=-1, keepdims=True) * v_h
    y = y + bonus.reshape(bq, t, MIX_W)
    return y * g, st, f[:, -1]


def _mixers(proj, l, pos, ret_chunk, h_rg, conv_rg, s_ret, s_rwkv, shift_rwkv, p):
    a_u, a_v, b_x, b_g, c_q, c_k, c_v, c_g, d_f = _split_cols(proj, IN_SIZES)
    y_a, vr = _chunk_gating(a_u, a_v, p['a_ln_g'][l], p['a_ln_b'][l], p['a_ws'][l], p['a_bs'][l])
    y_b, hb, cb = _rg_lru(b_x, b_g, h_rg, conv_rg, p['b_conv_w'][l], p['b_conv_b'][l],
                          p['b_wr'][l], p['b_br'][l], p['b_wi'][l], p['b_bi'][l], p['b_lambda'][l])
    y_c, sc = _retention(c_q, c_k, c_v, c_g, pos, s_ret, ret_chunk, p['c_gn_g'][l], p['c_gn_b'][l])
    y_d, sd, shd = _rwkv7(d_f, shift_rwkv, s_rwkv, p['d_mu'][l], p['d_w0'][l], p['d_w2'][l],
                          p['d_a0'][l], p['d_a2'][l], p['d_g2'][l], p['d_kk'][l], p['d_ka'][l],
                          p['d_rk'][l], p['d_lnx_g'][l], p['d_lnx_b'][l])
    return jnp.concatenate([y_a, y_b, y_c, y_d], axis=-1), (vr, hb, cb, sc, sd, shd)


def kernel(x_prompt, x_sample, c_prompt, c_sample, state_rglru_h, state_rglru_conv, state_retention, state_rwkv, state_rwkv_shift, w_ada, b_ada, ln_post_g, ln_post_b, w_in, w_out, a_ln_g, a_ln_b, a_ws, a_bs, b_conv_w, b_conv_b, b_wr, b_br, b_wi, b_bi, b_lambda, c_gn_g, c_gn_b, d_mu, d_w0, d_w2, d_a0, d_a2, d_g2, d_kk, d_ka, d_rk, d_lnx_g, d_lnx_b, w_router, b_router, w_up, b_up, w_down, b_down):
    p = dict(a_ln_g=a_ln_g, a_ln_b=a_ln_b, a_ws=a_ws, a_bs=a_bs, b_conv_w=b_conv_w, b_conv_b=b_conv_b,
             b_wr=b_wr, b_br=b_br, b_wi=b_wi, b_bi=b_bi, b_lambda=b_lambda, c_gn_g=c_gn_g, c_gn_b=c_gn_b,
             d_mu=d_mu, d_w0=d_w0, d_w2=d_w2, d_a0=d_a0, d_a2=d_a2, d_g2=d_g2, d_kk=d_kk, d_ka=d_ka,
             d_rk=d_rk, d_lnx_g=d_lnx_g, d_lnx_b=d_lnx_b)
    bp, tp, d = x_prompt.shape
    bs, ts, _ = x_sample.shape
    np_tok = bp * tp
    ns_tok = bs * ts

    c_act = jax.nn.silu(jnp.concatenate([c_prompt, c_sample], axis=0)).astype(BF16)

    def modulation(l, i):
        mod = _mm(c_act, w_ada[l, i]) + b_ada[l, i]
        shift, scale, gate = jnp.split(mod, 3, axis=-1)

        def rows(m):
            return jnp.concatenate([jnp.repeat(m[:bp], tp, axis=0), jnp.repeat(m[bp:], ts, axis=0)], axis=0)

        return rows(shift), rows(scale), rows(1.0 + gate)

    zeros = lambda s: jnp.zeros((bp,) + s.shape[1:], s.dtype)
    st_p = (zeros(state_rglru_h), zeros(state_rglru_conv), zeros(state_retention), zeros(state_rwkv),
            zeros(state_rwkv_shift))
    st_s = (state_rglru_h, state_rglru_conv, state_retention, state_rwkv, state_rwkv_shift)
    pos_p = jnp.arange(tp, dtype=jnp.int32)
    pos_s = PAST_LEN + jnp.arange(ts, dtype=jnp.int32)

    bg = b_up[:, :, 0::2].reshape(DEPTH, N_EXPERTS, 1, D_FF)
    bl = b_up[:, :, 1::2].reshape(DEPTH, N_EXPERTS, 1, D_FF)
    bd = b_down.reshape(DEPTH, N_EXPERTS, 1, d)

    x = jnp.concatenate([x_prompt.reshape(np_tok, d), x_sample.reshape(ns_tok, d)], axis=0)
    outs_p, outs_s = [], []
    for l in range(DEPTH):
        shift, scale, gate = modulation(l, 0)
        h = (_layer_norm(x) * (1.0 + scale) + shift).astype(BF16)
        proj = _mm(h, w_in[l])
        mix_p, new_p = _mixers(proj[:np_tok].reshape(bp, tp, -1), l, pos_p, min(CHUNK, tp),
                               *(s[:, l] for s in st_p), p)
        mix_s, new_s = _mixers(proj[np_tok:].reshape(bs, ts, -1), l, pos_s, ts,
                               *(s[:, l] for s in st_s), p)
        outs_p.append(new_p)
        outs_s.append(new_s)
        mix_in = jnp.concatenate([mix_p.reshape(np_tok, d), mix_s.reshape(ns_tok, d)], axis=0).astype(BF16)
        mix = _mm(mix_in, w_out[l])
        x = _layer_norm(DEEPNORM_ALPHA * x + gate * mix) * ln_post_g[l, 0] + ln_post_b[l, 0]

        shift, scale, gate = modulation(l, 1)
        h = _layer_norm(x) * (1.0 + scale) + shift
        ffn = _moe_ffn(h, w_router[l], b_router[l], w_up, bg, bl, w_down, bd, l)
        x = _layer_norm(DEEPNORM_ALPHA * x + gate * ffn) * ln_post_g[l, 1] + ln_post_b[l, 1]

    def stack(outs, j):
        return jnp.stack([o[j] for o in outs], axis=1)

    y_prompt = x[:np_tok].reshape(bp, tp, d)
    y_sample = x[np_tok:].reshape(bs, ts, d)
    return (y_prompt, y_sample) + tuple(stack(outs_p, j) for j in range(6)) + tuple(stack(outs_s, j) for j in range(6))
```

```python
import functools

import jax
import jax.numpy as jnp
from jax import lax
import numpy as np
from jax.experimental import pallas as pl
from jax.experimental.pallas import tpu as pltpu

F32 = jnp.float32
BF16 = jnp.bfloat16

D_MODEL = 4096
DEPTH = 2
MIX_W = D_MODEL // 4
CHUNK = 128
A_HD = 128
A_HEADS = MIX_W // A_HD
B_BLOCKS = 8
B_BS = MIX_W // B_BLOCKS
CONV_W = 4
RG_C = 8.0
C_HEADS = 4
C_DK = MIX_W // 8
C_DV = MIX_W // C_HEADS
ROPE_BASE = 10000.0
PAST_LEN = 16384
D_HD = 64
D_HEADS = MIX_W // D_HD
D_DECAY_LORA = 64
D_AAA_LORA = 64
D_GATE_LORA = 160
D_COLS = 3 * MIX_W + D_DECAY_LORA + D_AAA_LORA + D_GATE_LORA
IN_SIZES = (MIX_W, MIX_W, MIX_W, MIX_W, C_HEADS * C_DK, C_HEADS * C_DK, MIX_W, MIX_W, D_COLS)
D_SIZES = (MIX_W, MIX_W, MIX_W, D_DECAY_LORA, D_AAA_LORA, D_GATE_LORA)
N_EXPERTS = 32
TOP_K = 4
D_FF = D_MODEL
SWIGLU_LIMIT = 7.0
SWIGLU_ALPHA = 1.702
DEEPNORM_ALPHA = (2.0 * DEPTH) ** 0.25
LN_EPS = 1e-5
RWKV_LN_EPS = 64e-5

VMEM_LIMIT_BYTES = 56 * 1024 * 1024

LANES = 128
SUBLANES = 8

MM_TM = 512
MM_TN = 512
MOE_TM = 256
MOE_TF = 512
MOE_TN = 1024
PREP_ROWS = 512
RWKV_TC = 64
RWKV_BB = 4
LRU_TC = 128


def _split_cols(t, sizes):
    return jnp.split(t, np.cumsum(sizes)[:-1].tolist(), axis=-1)


def _layer_norm(x, eps=LN_EPS):
    mu = jnp.mean(x, axis=-1, keepdims=True)
    var = jnp.mean(jnp.square(x - mu), axis=-1, keepdims=True)
    return (x - mu) * lax.rsqrt(var + eps)


def _mm_kernel(x_ref, w_ref, o_ref, wbf_ref):
    @pl.when(pl.program_id(1) == 0)
    def _():
        wbf_ref[...] = w_ref[...].astype(BF16)

    o_ref[...] = jnp.dot(x_ref[...], wbf_ref[...], preferred_element_type=F32)


def _mm(x, w, lead=(), tm=MM_TM, tn=MM_TN):
    m, k = x.shape
    n = w.shape[-1]
    tm = min(tm, m)
    return pl.pallas_call(
        _mm_kernel,
        grid=(pl.cdiv(n, tn), pl.cdiv(m, tm)),
        in_specs=[pl.BlockSpec((tm, k), lambda j, i: (i, 0)),
                  pl.BlockSpec((None,) * len(lead) + (k, tn), lambda j, i: tuple(lead) + (0, j))],
        out_specs=pl.BlockSpec((tm, tn), lambda j, i: (i, j)),
        out_shape=jax.ShapeDtypeStruct((m, n), F32),
        scratch_shapes=[pltpu.VMEM((k, tn), BF16)],
        compiler_params=pltpu.CompilerParams(
            dimension_semantics=("arbitrary", "arbitrary"),
            vmem_limit_bytes=VMEM_LIMIT_BYTES),
    )(x, w)


def _router_kernel(x_ref, w_ref, b_ref, o_ref):
    o_ref[...] = jnp.dot(x_ref[...], w_ref[...], preferred_element_type=F32,
                         precision=lax.Precision.HIGHEST) + b_ref[...]


def _router(tok, w, b, tm=512):
    m, k = tok.shape
    n = w.shape[1]
    return pl.pallas_call(
        _router_kernel,
        grid=(m // tm,),
        in_specs=[pl.BlockSpec((tm, k), lambda i: (i, 0)),
                  pl.BlockSpec((k, n), lambda i: (0, 0)),
                  pl.BlockSpec((1, n), lambda i: (0, 0))],
        out_specs=pl.BlockSpec((tm, n), lambda i: (i, 0)),
        out_shape=jax.ShapeDtypeStruct((m, n), F32),
        compiler_params=pltpu.CompilerParams(
            dimension_semantics=("arbitrary",),
            vmem_limit_bytes=VMEM_LIMIT_BYTES),
    )(tok, w, b.reshape(1, n))


def _moe_up_kernel(be_ref, bn_ref, bf_ref, bs_ref, nv_ref, cv_ref,
                   x_ref, w_ref, bg_ref, bl_ref, act_ref, wsc_ref):
    b = pl.program_id(1)
    d = w_ref.shape[2]
    n_groups = w_ref.shape[3] // (2 * LANES)

    @pl.when(nv_ref[b] > 0)
    def _():
        gu = jnp.dot(x_ref[...], wsc_ref[...], preferred_element_type=F32)
        for c in range(n_groups):
            g = gu[:, c * 2 * LANES:c * 2 * LANES + LANES] + bg_ref[0, 0, :, c * LANES:(c + 1) * LANES]
            u = gu[:, c * 2 * LANES + LANES:(c + 1) * 2 * LANES] + bl_ref[0, 0, :, c * LANES:(c + 1) * LANES]
            g = jnp.minimum(g, SWIGLU_LIMIT)
            u = jnp.clip(u, -SWIGLU_LIMIT, SWIGLU_LIMIT)
            act = g * jax.nn.sigmoid(SWIGLU_ALPHA * g) * (u + 1.0)
            act_ref[:, c * LANES:(c + 1) * LANES] = act.astype(BF16)

    @pl.when(nv_ref[b] == 0)
    def _():
        act_ref[...] = jnp.zeros(act_ref.shape, act_ref.dtype)

    @pl.when(cv_ref[b] == 1)
    def _():
        ii = lax.broadcasted_iota(jnp.int32, (2 * LANES, 2 * LANES), 0)
        jj = lax.broadcasted_iota(jnp.int32, (2 * LANES, 2 * LANES), 1)
        src = jnp.where(jj < LANES, 2 * jj, 2 * (jj - LANES) + 1)
        perm = jnp.where(ii == src, 1.0, 0.0).astype(BF16)

        def body(r, carry):
            rows = pl.ds(pl.multiple_of(r * PREP_ROWS, PREP_ROWS), PREP_ROWS)
            for c in range(n_groups):
                cols = slice(c * 2 * LANES, (c + 1) * 2 * LANES)
                wb = w_ref[0, 0, rows, cols].astype(BF16)
                wsc_ref[rows, cols] = jnp.dot(wb, perm, preferred_element_type=F32).astype(BF16)
            return carry

        lax.fori_loop(0, d // PREP_ROWS, body, 0)


def _moe_down_kernel(be_ref, bn_ref, bf_ref, bs_ref, nv_ref, cv_ref, a_ref, w_ref, bd_ref, o_ref, wsc_ref):
    b = pl.program_id(1)
    f = w_ref.shape[2]

    @pl.when(nv_ref[b] > 0)
    def _():
        o_ref[...] = jnp.dot(a_ref[...], wsc_ref[...], preferred_element_type=F32) + bd_ref[0, 0]

    @pl.when(nv_ref[b] == 0)
    def _():
        o_ref[...] = jnp.zeros(o_ref.shape, o_ref.dtype)

    @pl.when(cv_ref[b] == 1)
    def _():
        def body(r, carry):
            rows = pl.ds(pl.multiple_of(r * PREP_ROWS, PREP_ROWS), PREP_ROWS)
            wsc_ref[rows, :] = w_ref[0, 0, rows, :].astype(BF16)
            return carry

        lax.fori_loop(0, f // PREP_ROWS, body, 0)


def _moe_experts(xs, sched, w_up, bg, bl, w_down, b_down, layer, tm, tf, tn):
    rows, d = xs.shape
    nb = rows // tm
    f_dim = w_down.shape[2]
    nf = f_dim // tf
    nn = d // tn
    params = pltpu.CompilerParams(dimension_semantics=("arbitrary", "arbitrary"),
                                  vmem_limit_bytes=VMEM_LIMIT_BYTES)
    up_spec = pltpu.PrefetchScalarGridSpec(
        num_scalar_prefetch=6,
        grid=(nf, nb),
        in_specs=[
            pl.BlockSpec((tm, d), lambda f, b, be, bn, bf, bs, nv, cv: (bs[b], 0)),
            pl.BlockSpec((1, 1, d, 2 * tf),
                         lambda f, b, be, bn, bf, bs, nv, cv: (layer, bn[b], 0, jnp.minimum(f + bf[b], nf - 1))),
            pl.BlockSpec((1, 1, 1, tf), lambda f, b, be, bn, bf, bs, nv, cv: (layer, be[b], 0, f)),
            pl.BlockSpec((1, 1, 1, tf), lambda f, b, be, bn, bf, bs, nv, cv: (layer, be[b], 0, f)),
        ],
        out_specs=pl.BlockSpec((tm, tf), lambda f, b, be, bn, bf, bs, nv, cv: (b, f)),
        scratch_shapes=[pltpu.VMEM((d, 2 * tf), BF16)],
    )
    act = pl.pallas_call(
        _moe_up_kernel, grid_spec=up_spec,
        out_shape=jax.ShapeDtypeStruct((rows, f_dim), BF16),
        compiler_params=params, name="moe_up",
    )(*sched, xs, w_up, bg, bl)
    down_spec = pltpu.PrefetchScalarGridSpec(
        num_scalar_prefetch=6,
        grid=(nn, nb),
        in_specs=[
            pl.BlockSpec((tm, f_dim), lambda n, b, be, bn, bf, bs, nv, cv: (bs[b], 0)),
            pl.BlockSpec((1, 1, f_dim, tn),
                         lambda n, b, be, bn, bf, bs, nv, cv: (layer, bn[b], 0, jnp.minimum(n + bf[b], nn - 1))),
            pl.BlockSpec((1, 1, 1, tn), lambda n, b, be, bn, bf, bs, nv, cv: (layer, be[b], 0, n)),
        ],
        out_specs=pl.BlockSpec((tm, tn), lambda n, b, be, bn, bf, bs, nv, cv: (b, n)),
        scratch_shapes=[pltpu.VMEM((f_dim, tn), BF16)],
    )
    return pl.pallas_call(
        _moe_down_kernel, grid_spec=down_spec,
        out_shape=jax.ShapeDtypeStruct((rows, d), F32),
        compiler_params=params, name="moe_down",
    )(*sched, act, w_down, b_down)


def _moe_ffn(tok, w_router, b_router, w_up, bg, bl, w_down, b_down, layer,
             tm=MOE_TM, tf=MOE_TF, tn=MOE_TN):
    n_tok = tok.shape[0]
    n_exp = w_router.shape[1]
    logits = _router(tok, w_router, b_router)
    top_logit, top_idx = lax.top_k(logits, TOP_K)
    gates = jax.nn.softmax(top_logit, axis=-1)
    n_assign = n_tok * TOP_K
    nb = -(-n_assign // tm) + n_exp
    flat_e = top_idx.reshape(-1).astype(jnp.int32)
    onehot = (flat_e[:, None] == jnp.arange(n_exp, dtype=jnp.int32)[None, :]).astype(jnp.int32)
    csum = jnp.cumsum(onehot, axis=0)
    counts = csum[-1]
    rank = jnp.take_along_axis(csum, flat_e[:, None], axis=1)[:, 0] - 1
    blocks_e = (counts + tm - 1) // tm
    blk_end = jnp.cumsum(blocks_e)
    blk_start = blk_end - blocks_e
    n_used = blk_end[-1]
    dest = (blk_start[flat_e] + 1) * tm + rank
    src_tok = jnp.arange(n_assign, dtype=jnp.int32) // TOP_K
    row_tok = jnp.zeros(((nb + 1) * tm,), jnp.int32).at[dest].set(src_tok)
    ids = jnp.minimum(jnp.arange(nb, dtype=jnp.int32), n_used - 1)
    own = jnp.minimum(jnp.searchsorted(blk_end, ids, side='right'), n_exp - 1).astype(jnp.int32)
    own_end = blk_end[own]
    is_last_expert = own_end >= n_used
    after = jnp.minimum(jnp.searchsorted(blk_end, jnp.minimum(own_end, n_used - 1), side='right'),
                        n_exp - 1).astype(jnp.int32)
    staged = jnp.where(is_last_expert, own[0], after)
    valid = jnp.arange(nb, dtype=jnp.int32) < n_used
    refill = valid & (ids == own_end - 1) & ~is_last_expert
    one = jnp.ones((1,), jnp.int32)
    zero = jnp.zeros((1,), jnp.int32)
    sched = (jnp.concatenate([own[:1], own]),
             jnp.concatenate([own[:1], staged]),
             jnp.concatenate([zero, is_last_expert.astype(jnp.int32)]),
             jnp.concatenate([one, ids + 1]),
             jnp.concatenate([zero, valid.astype(jnp.int32)]),
             jnp.concatenate([one, refill.astype(jnp.int32)]))
    xs = tok[row_tok].astype(BF16)
    ys = _moe_experts(xs, sched, w_up, bg, bl, w_down, b_down, layer, tm, tf, tn)
    picked = ys[dest.reshape(n_tok, TOP_K).T.reshape(-1)].reshape(TOP_K, n_tok, -1)
    return jnp.sum(picked * gates.T[:, :, None], axis=0)


def _segsum(p, ones_blk, split):
    hi = p.astype(BF16)
    parts = [hi]
    if split:
        parts.append((p - hi.astype(F32)).astype(BF16))
    n_seg = p.shape[1] // (2 * LANES)
    stacked = jnp.concatenate([x[:, j * 2 * LANES:(j + 1) * 2 * LANES] for x in parts for j in range(n_seg)], axis=0)
    r = jnp.dot(stacked, ones_blk, preferred_element_type=F32)
    rows = p.shape[0] * n_seg
    out = r[:rows]
    if split:
        out = out + r[rows:]
    return jnp.concatenate([out[j * p.shape[0]:(j + 1) * p.shape[0]] for j in range(n_seg)], axis=1)


def _rwkv_kernel(kk_ref, z_ref, w_ref, b_ref, k_ref, vt_ref, s0_ref, y_ref, st_ref, s_scr):
    nb, tc, width = kk_ref.shape
    hd = s_scr.shape[1]

    @pl.when(pl.program_id(1) == 0)
    def _():
        s_scr[...] = s0_ref[...]

    ri = lax.broadcasted_iota(jnp.int32, (2 * LANES, 2 * LANES), 0)
    ci = lax.broadcasted_iota(jnp.int32, (2 * LANES, 2 * LANES), 1)
    ones_blk = jnp.where((ri // hd) == (ci // hd), 1.0, 0.0).astype(BF16)
    er = lax.broadcasted_iota(jnp.int32, (LANES, width), 0)
    ec = lax.broadcasted_iota(jnp.int32, (LANES, width), 1)
    n_heads = width // hd
    expand = jnp.where(jnp.where(er < 3 * n_heads, er % n_heads, -1) == (ec // hd), 1.0, 0.0).astype(BF16)
    dr = lax.broadcasted_iota(jnp.int32, (hd, width), 0)
    dc = lax.broadcasted_iota(jnp.int32, (hd, width), 1)
    diag = (dc % hd) == dr

    def step(t, carry):
        for b in range(nb):
            s = s_scr[b]
            sa = _segsum(s * kk_ref[b, pl.ds(t, 1), :], ones_blk, True)
            q = _segsum(s * z_ref[b, pl.ds(t, 1), :], ones_blk, False)
            ve = jnp.dot(vt_ref[b, t], expand, preferred_element_type=F32)
            s_scr[b] = s * w_ref[b, pl.ds(t, 1), :] - sa * b_ref[b, pl.ds(t, 1), :] + ve * k_ref[b, pl.ds(t, 1), :]
            yd = jnp.where(diag, q, 0.0)
            y_ref[b, t] = jnp.sum(yd.reshape(hd // SUBLANES, SUBLANES, width), axis=0)
        return carry

    lax.fori_loop(0, tc, step, 0)
    st_ref[...] = s_scr[...]


def _rwkv_scan(kk, z, w, bb, k, vt, s0, tc):
    nb, t, width = kk.shape
    hd = s0.shape[1]
    gb = RWKV_BB
    row = pl.BlockSpec((gb, tc, width), lambda bi, ti: (bi, ti, 0))
    state = pl.BlockSpec((gb, hd, width), lambda bi, ti: (bi, 0, 0))
    return pl.pallas_call(
        _rwkv_kernel,
        grid=(nb // gb, t // tc),
        in_specs=[row, row, row, row, row,
                  pl.BlockSpec((gb, tc, hd, LANES), lambda bi, ti: (bi, ti, 0, 0)),
                  state],
        out_specs=[pl.BlockSpec((gb, tc, SUBLANES, width), lambda bi, ti: (bi, ti, 0, 0)),
                   state],
        out_shape=[jax.ShapeDtypeStruct((nb, t, SUBLANES, width), F32),
                   jax.ShapeDtypeStruct((nb, hd, width), F32)],
        scratch_shapes=[pltpu.VMEM((gb, hd, width), F32)],
        compiler_params=pltpu.CompilerParams(dimension_semantics=("arbitrary", "arbitrary"),
                                             vmem_limit_bytes=VMEM_LIMIT_BYTES),
        name="rwkv_scan",
    )(kk, z, w, bb, k, vt, s0)


def _lru_kernel(a_ref, x_ref, h0_ref, hs_ref, h_scr):
    @pl.when(pl.program_id(0) == 0)
    def _():
        h_scr[...] = h0_ref[...]

    def step(t, h):
        h = a_ref[t] * h + x_ref[t]
        hs_ref[t] = h
        return h

    h_scr[...] = lax.fori_loop(0, a_ref.shape[0], step, h_scr[...], unroll=8)


def _lru_scan(a, x, h0, tc=LRU_TC):
    t, nb, width = a.shape
    blk = pl.BlockSpec((tc, nb, width), lambda i: (i, 0, 0))
    return pl.pallas_call(
        _lru_kernel,
        grid=(t // tc,),
        in_specs=[blk, blk, pl.BlockSpec((nb, width), lambda i: (0, 0))],
        out_specs=blk,
        out_shape=jax.ShapeDtypeStruct((t, nb, width), F32),
        scratch_shapes=[pltpu.VMEM((nb, width), F32)],
        compiler_params=pltpu.CompilerParams(dimension_semantics=("arbitrary",),
                                             vmem_limit_bytes=VMEM_LIMIT_BYTES),
        name="lru_scan",
    )(a, x, h0)


def _rotary(x, pos):
    half = x.shape[-1] // 2
    inv = ROPE_BASE ** (-jnp.arange(half, dtype=F32) / half)
    ang = pos.astype(F32)[:, None] * inv[None, :]
    cos = jnp.cos(ang)[None, :, None, :]
    sin = jnp.sin(ang)[None, :, None, :]
    x1 = x[..., :half]
    x2 = x[..., half:]
    return jnp.concatenate([x1 * cos - x2 * sin, x1 * sin + x2 * cos], axis=-1)


def _chunk_gating(u, v, ln_g, ln_b, ws, bs):
    bq, t, _ = u.shape
    l = min(t, CHUNK)
    n = t // l
    vn = _layer_norm(v) * ln_g + ln_b
    mask = jnp.tril(jnp.ones((l, l), dtype=bool))
    w = jnp.where(mask[None], ws[:, :l, :l], 0.0)
    vh = vn.reshape(bq, n, l, A_HEADS, A_HD)
    mixed = jnp.einsum('hts,bnshd->bnthd', w, vh) + bs[:, :l].T[None, None, :, :, None]
    return u * mixed.reshape(bq, t, MIX_W), vn[:, t - l:]


def _rg_lru(xb, gb, h0, conv_buf, conv_w, conv_b, w_r, b_r, w_i, b_i, lam):
    bq, t, _ = xb.shape
    xpad = jnp.concatenate([conv_buf, xb], axis=1)
    xc = conv_b + sum(xpad[:, j:j + t] * conv_w[j] for j in range(CONV_W))
    xg = xc.reshape(bq, t, B_BLOCKS, B_BS)
    r = jax.nn.sigmoid(jnp.einsum('btnc,ncd->btnd', xg, w_r).reshape(bq, t, MIX_W) + b_r)
    i = jax.nn.sigmoid(jnp.einsum('btnc,ncd->btnd', xg, w_i).reshape(bq, t, MIX_W) + b_i)
    log_a = -RG_C * r * jax.nn.softplus(-lam)
    a = jnp.exp(log_a)
    xin = jnp.sqrt(-jnp.expm1(2.0 * log_a)) * (i * xc)

    hs = jnp.moveaxis(_lru_scan(jnp.moveaxis(a, 1, 0), jnp.moveaxis(xin, 1, 0), h0, min(LRU_TC, t)), 0, 1)
    y = hs * jax.nn.gelu(gb, approximate=True)
    return y, hs[:, -1], xpad[:, -(CONV_W - 1):]


def _retention(q, k, v, g, pos, s0, chunk, gn_g, gn_b):
    bq, t, _ = q.shape
    qh = _rotary(q.reshape(bq, t, C_HEADS, C_DK), pos)
    kh = _rotary(k.reshape(bq, t, C_HEADS, C_DK), pos) * (C_DK ** -0.5)
    vh = v.reshape(bq, t, C_HEADS, C_DV)
    lg = jnp.log(1.0 - 2.0 ** (-5.0 - jnp.arange(C_HEADS, dtype=F32)))
    n = t // chunk
    idx = jnp.arange(chunk, dtype=F32)
    diff = idx[:, None] - idx[None, :]
    intra = jnp.where(diff[None] >= 0, jnp.exp(jnp.maximum(diff, 0.0)[None] * lg[:, None, None]), 0.0)
    q_decay = jnp.exp((idx[:, None] + 1.0) * lg[None, :])
    k_decay = jnp.exp((chunk - 1.0 - idx[:, None]) * lg[None, :])
    chunk_decay = jnp.exp(chunk * lg)

    def to_blocks(x):
        return jnp.moveaxis(x.reshape(bq, n, chunk, C_HEADS, x.shape[-1]), 1, 0)

    def step(s, blk):
        qb, kb, vb = blk
        sc = jnp.einsum('bihd,bjhd->bhij', qb, kb) * intra[None]
        y = jnp.einsum('bhij,bjhe->bihe', sc, vb) + jnp.einsum('bihd,bhde->bihe', qb, s) * q_decay[None, :, :, None]
        s = s * chunk_decay[None, :, None, None] + jnp.einsum('bjhd,bjhe->bhde', kb * k_decay[None, :, :, None], vb)
        return s, y

    st, ys = lax.scan(step, s0, (to_blocks(qh), to_blocks(kh), to_blocks(vh)))
    y = jnp.moveaxis(ys, 0, 1).reshape(bq, t, C_HEADS, C_DV)
    y = _layer_norm(y).reshape(bq, t, MIX_W) * gn_g + gn_b
    return jax.nn.silu(g) * y, st


def _split3(x):
    hi = x.astype(BF16)
    r1 = x - hi.astype(F32)
    mid = r1.astype(BF16)
    lo = (r1 - mid.astype(F32)).astype(BF16)
    return hi, mid, lo


def _rwkv7(f, shift_buf, s0, mu, w0, w2, a0, a2, g2, k_k, k_a, r_k, lnx_g, lnx_b):
    bq, t, _ = f.shape
    prev = jnp.concatenate([shift_buf[:, None, :], f[:, :-1]], axis=1)
    fs = f + mu * (prev - f)
    r, k, v, fw, fa, fg = _split_cols(fs, D_SIZES)
    log_w = -jnp.exp(-jax.nn.softplus(-(w0 + jnp.tanh(fw) @ w2)) - 0.5)
    a = jax.nn.sigmoid(a0 + fa @ a2)
    g = jax.nn.sigmoid(fg) @ g2

    def heads(x):
        return x.reshape(bq, t, D_HEADS, D_HD)

    kk = heads(k * k_k)
    kk = kk * lax.rsqrt(jnp.maximum(jnp.sum(kk * kk, axis=-1, keepdims=True), 1e-24))
    k_h = heads(k * (1.0 + (a - 1.0) * k_a))
    r_h, v_h, w_h, a_h = heads(r), heads(v), heads(log_w), heads(a)

    def flat(x):
        return x.reshape(bq, t, MIX_W)

    def head_sum(x):
        return jnp.broadcast_to(jnp.sum(x, axis=-1, keepdims=True), x.shape)

    decay = jnp.exp(w_h)
    b_h = kk * a_h
    z_h = decay * r_h - kk * head_sum(b_h * r_h)
    vt = jnp.concatenate(_split3(jnp.swapaxes(v_h, 2, 3)), axis=-1)
    vt = jnp.pad(vt, ((0, 0), (0, 0), (0, 0), (0, LANES - vt.shape[-1])))
    s0k = jnp.swapaxes(s0, 1, 2).reshape(bq, D_HD, MIX_W)
    y8, st = _rwkv_scan(flat(kk), flat(z_h), flat(decay), flat(b_h), flat(k_h), vt, s0k, min(RWKV_TC, t))
    y = jnp.sum(y8, axis=2).reshape(bq, t, D_HEADS, D_HD) + head_sum(k_h * r_h) * v_h
    st = jnp.swapaxes(st.reshape(bq, D_HD, D_HEADS, D_HD), 1, 2)
    y = _layer_norm(y, RWKV_LN_EPS).reshape(bq, t, MIX_W) * lnx_g + lnx_b
    bonus = jnp.sum(r_h * k_h * r_k, axis=-1, keepdims=True) * v_h
    y = y + bonus.reshape(bq, t, MIX_W)
    return y * g, st, f[:, -1]


def _mixers(proj, l, pos, ret_chunk, h_rg, conv_rg, s_ret, s_rwkv, shift_rwkv, p):
    a_u, a_v, b_x, b_g, c_q, c_k, c_v, c_g, d_f = _split_cols(proj, IN_SIZES)
    y_a, vr = _chunk_gating(a_u, a_v, p['a_ln_g'][l], p['a_ln_b'][l], p['a_ws'][l], p['a_bs'][l])
    y_b, hb, cb = _rg_lru(b_x, b_g, h_rg, conv_rg, p['b_conv_w'][l], p['b_conv_b'][l],
                          p['b_wr'][l], p['b_br'][l], p['b_wi'][l], p['b_bi'][l], p['b_lambda'][l])
    y_c, sc = _retention(c_q, c_k, c_v, c_g, pos, s_ret, ret_chunk, p['c_gn_g'][l], p['c_gn_b'][l])
    y_d, sd, shd = _rwkv7(d_f, shift_rwkv, s_rwkv, p['d_mu'][l], p['d_w0'][l], p['d_w2'][l],
                          p['d_a0'][l], p['d_a2'][l], p['d_g2'][l], p['d_kk'][l], p['d_ka'][l],
                          p['d_rk'][l], p['d_lnx_g'][l], p['d_lnx_b'][l])
    return jnp.concatenate([y_a, y_b, y_c, y_d], axis=-1), (vr, hb, cb, sc, sd, shd)


def kernel(x_prompt, x_sample, c_prompt, c_sample, state_rglru_h, state_rglru_conv, state_retention, state_rwkv, state_rwkv_shift, w_ada, b_ada, ln_post_g, ln_post_b, w_in, w_out, a_ln_g, a_ln_b, a_ws, a_bs, b_conv_w, b_conv_b, b_wr, b_br, b_wi, b_bi, b_lambda, c_gn_g, c_gn_b, d_mu, d_w0, d_w2, d_a0, d_a2, d_g2, d_kk, d_ka, d_rk, d_lnx_g, d_lnx_b, w_router, b_router, w_up, b_up, w_down, b_down):
    p = dict(a_ln_g=a_ln_g, a_ln_b=a_ln_b, a_ws=a_ws, a_bs=a_bs, b_conv_w=b_conv_w, b_conv_b=b_conv_b,
             b_wr=b_wr, b_br=b_br, b_wi=b_wi, b_bi=b_bi, b_lambda=b_lambda, c_gn_g=c_gn_g, c_gn_b=c_gn_b,
             d_mu=d_mu, d_w0=d_w0, d_w2=d_w2, d_a0=d_a0, d_a2=d_a2, d_g2=d_g2, d_kk=d_kk, d_ka=d_ka,
             d_rk=d_rk, d_lnx_g=d_lnx_g, d_lnx_b=d_lnx_b)
    bp, tp, d = x_prompt.shape
    bs, ts, _ = x_sample.shape
    np_tok = bp * tp
    ns_tok = bs * ts

    c_act = jax.nn.silu(jnp.concatenate([c_prompt, c_sample], axis=0)).astype(BF16)

    def modulation(l, i):
        mod = _mm(c_act, w_ada, (l, i)) + b_ada[l, i]
        shift, scale, gate = jnp.split(mod, 3, axis=-1)

        def rows(m):
            return jnp.concatenate([jnp.repeat(m[:bp], tp, axis=0), jnp.repeat(m[bp:], ts, axis=0)], axis=0)

        return rows(shift), rows(scale), rows(1.0 + gate)

    zeros = lambda s: jnp.zeros((bp,) + s.shape[1:], s.dtype)
    st_p = (zeros(state_rglru_h), zeros(state_rglru_conv), zeros(state_retention), zeros(state_rwkv),
            zeros(state_rwkv_shift))
    st_s = (state_rglru_h, state_rglru_conv, state_retention, state_rwkv, state_rwkv_shift)
    pos_p = jnp.arange(tp, dtype=jnp.int32)
    pos_s = PAST_LEN + jnp.arange(ts, dtype=jnp.int32)

    bg = b_up[:, :, 0::2].reshape(DEPTH, N_EXPERTS, 1, D_FF)
    bl = b_up[:, :, 1::2].reshape(DEPTH, N_EXPERTS, 1, D_FF)
    bd = b_down.reshape(DEPTH, N_EXPERTS, 1, d)

    x = jnp.concatenate([x_prompt.reshape(np_tok, d), x_sample.reshape(ns_tok, d)], axis=0)
    outs_p, outs_s = [], []
    for l in range(DEPTH):
        shift, scale, gate = modulation(l, 0)
        h = (_layer_norm(x) * (1.0 + scale) + shift).astype(BF16)
        proj = _mm(h, w_in, (l,))
        mix_p, new_p = _mixers(proj[:np_tok].reshape(bp, tp, -1), l, pos_p, min(CHUNK, tp),
                               *(s[:, l] for s in st_p), p)
        mix_s, new_s = _mixers(proj[np_tok:].reshape(bs, ts, -1), l, pos_s, ts,
                               *(s[:, l] for s in st_s), p)
        outs_p.append(new_p)
        outs_s.append(new_s)
        mix_in = jnp.concatenate([mix_p.reshape(np_tok, d), mix_s.reshape(ns_tok, d)], axis=0).astype(BF16)
        mix = _mm(mix_in, w_out, (l,))
        x = _layer_norm(DEEPNORM_ALPHA * x + gate * mix) * ln_post_g[l, 0] + ln_post_b[l, 0]

        shift, scale, gate = modulation(l, 1)
        h = _layer_norm(x) * (1.0 + scale) + shift
        ffn = _moe_ffn(h, w_router[l], b_router[l], w_up, bg, bl, w_down, bd, l)
        x = _layer_norm(DEEPNORM_ALPHA * x + gate * ffn) * ln_post_g[l, 1] + ln_post_b[l, 1]

    def stack(outs, j):
        return jnp.stack([o[j] for o in outs], axis=1)

    y_prompt = x[:np_tok].reshape(bp, tp, d)
    y_sample = x[np_tok:].reshape(bs, ts, d)
    return (y_prompt, y_sample) + tuple(stack(outs_p, j) for j in range(6)) + tuple(stack(outs_s, j) for j in range(6))
```

```python
import functools

import jax
import jax.numpy as jnp
from jax import lax
import numpy as np
from jax.experimental import pallas as pl
from jax.experimental.pallas import tpu as pltpu

F32 = jnp.float32
BF16 = jnp.bfloat16

D_MODEL = 4096
DEPTH = 2
MIX_W = D_MODEL // 4
CHUNK = 128
A_HD = 128
A_HEADS = MIX_W // A_HD
B_BLOCKS = 8
B_BS = MIX_W // B_BLOCKS
CONV_W = 4
RG_C = 8.0
C_HEADS = 4
C_DK = MIX_W // 8
C_DV = MIX_W // C_HEADS
ROPE_BASE = 10000.0
PAST_LEN = 16384
D_HD = 64
D_HEADS = MIX_W // D_HD
D_DECAY_LORA = 64
D_AAA_LORA = 64
D_GATE_LORA = 160
D_COLS = 3 * MIX_W + D_DECAY_LORA + D_AAA_LORA + D_GATE_LORA
IN_SIZES = (MIX_W, MIX_W, MIX_W, MIX_W, C_HEADS * C_DK, C_HEADS * C_DK, MIX_W, MIX_W, D_COLS)
D_SIZES = (MIX_W, MIX_W, MIX_W, D_DECAY_LORA, D_AAA_LORA, D_GATE_LORA)
N_EXPERTS = 32
TOP_K = 4
D_FF = D_MODEL
SWIGLU_LIMIT = 7.0
SWIGLU_ALPHA = 1.702
DEEPNORM_ALPHA = (2.0 * DEPTH) ** 0.25
LN_EPS = 1e-5
RWKV_LN_EPS = 64e-5

VMEM_LIMIT_BYTES = 56 * 1024 * 1024

LANES = 128
SUBLANES = 8

MM_TM = 512
MM_TN = 512
MOE_TM = 256
MOE_TF = 512
MOE_TN = 1024
PREP_ROWS = 512
RWKV_TC = 64
RWKV_BB = 4
LRU_TC = 128


def _split_cols(t, sizes):
    return jnp.split(t, np.cumsum(sizes)[:-1].tolist(), axis=-1)


def _layer_norm(x, eps=LN_EPS):
    mu = jnp.mean(x, axis=-1, keepdims=True)
    var = jnp.mean(jnp.square(x - mu), axis=-1, keepdims=True)
    return (x - mu) * lax.rsqrt(var + eps)


def _mm_kernel(x_ref, w_ref, o_ref, wbf_ref):
    @pl.when(pl.program_id(1) == 0)
    def _():
        wbf_ref[...] = w_ref[...].astype(BF16)

    o_ref[...] = jnp.dot(x_ref[...], wbf_ref[...], preferred_element_type=F32)


def _mm(x, w, lead=(), tm=MM_TM, tn=MM_TN):
    m, k = x.shape
    n = w.shape[-1]
    tm = min(tm, m)
    return pl.pallas_call(
        _mm_kernel,
        grid=(pl.cdiv(n, tn), pl.cdiv(m, tm)),
        in_specs=[pl.BlockSpec((tm, k), lambda j, i: (i, 0)),
                  pl.BlockSpec((None,) * len(lead) + (k, tn), lambda j, i: tuple(lead) + (0, j))],
        out_specs=pl.BlockSpec((tm, tn), lambda j, i: (i, j)),
        out_shape=jax.ShapeDtypeStruct((m, n), F32),
        scratch_shapes=[pltpu.VMEM((k, tn), BF16)],
        compiler_params=pltpu.CompilerParams(
            dimension_semantics=("arbitrary", "arbitrary"),
            vmem_limit_bytes=VMEM_LIMIT_BYTES),
    )(x, w)


def _mm_split_kernel(x_ref, w_ref, oa_ref, ob_ref, wbf_ref, *, na):
    i = pl.program_id(1)

    @pl.when(i == 0)
    def _():
        wbf_ref[...] = w_ref[...].astype(BF16)

    r = jnp.dot(x_ref[...], wbf_ref[...], preferred_element_type=F32)

    @pl.when(i < na)
    def _():
        oa_ref[...] = r

    @pl.when(i >= na)
    def _():
        ob_ref[...] = r


def _mm_split(x, w, lead, m_a, tm=MM_TM, tn=MM_TN):
    m, k = x.shape
    n = w.shape[-1]
    na = m_a // tm
    nb = (m - m_a) // tm
    assert na * tm == m_a and nb * tm == m - m_a and na > 0 and nb > 0
    return pl.pallas_call(
        functools.partial(_mm_split_kernel, na=na),
        grid=(pl.cdiv(n, tn), na + nb),
        in_specs=[pl.BlockSpec((tm, k), lambda j, i: (i, 0)),
                  pl.BlockSpec((None,) * len(lead) + (k, tn), lambda j, i: tuple(lead) + (0, j))],
        out_specs=[pl.BlockSpec((tm, tn), lambda j, i: (jnp.minimum(i, na - 1), j)),
                   pl.BlockSpec((tm, tn), lambda j, i: (jnp.maximum(i - na, 0), j))],
        out_shape=[jax.ShapeDtypeStruct((m_a, n), F32), jax.ShapeDtypeStruct((m - m_a, n), F32)],
        scratch_shapes=[pltpu.VMEM((k, tn), BF16)],
        compiler_params=pltpu.CompilerParams(
            dimension_semantics=("arbitrary", "arbitrary"),
            vmem_limit_bytes=VMEM_LIMIT_BYTES),
    )(x, w)


def _router_kernel(x_ref, w_ref, b_ref, o_ref):
    o_ref[...] = jnp.dot(x_ref[...].astype(BF16), w_ref[...].astype(BF16),
                         preferred_element_type=F32) + b_ref[...]


def _router(tok, w, b, tm=512):
    m, k = tok.shape
    n = w.shape[1]
    return pl.pallas_call(
        _router_kernel,
        grid=(m // tm,),
        in_specs=[pl.BlockSpec((tm, k), lambda i: (i, 0)),
                  pl.BlockSpec((k, n), lambda i: (0, 0)),
                  pl.BlockSpec((1, n), lambda i: (0, 0))],
        out_specs=pl.BlockSpec((tm, n), lambda i: (i, 0)),
        out_shape=jax.ShapeDtypeStruct((m, n), F32),
        compiler_params=pltpu.CompilerParams(
            dimension_semantics=("arbitrary",),
            vmem_limit_bytes=VMEM_LIMIT_BYTES),
    )(tok, w, b.reshape(1, n))


def _moe_up_kernel(be_ref, bn_ref, bf_ref, bs_ref, nv_ref, cv_ref,
                   x_ref, w_ref, bg_ref, bl_ref, act_ref, wsc_ref):
    b = pl.program_id(1)
    d = w_ref.shape[2]
    n_groups = w_ref.shape[3] // (2 * LANES)

    @pl.when(nv_ref[b] > 0)
    def _():
        gu = jnp.dot(x_ref[...], wsc_ref[...], preferred_element_type=F32)
        for c in range(n_groups):
            g = gu[:, c * 2 * LANES:c * 2 * LANES + LANES] + bg_ref[0, 0, :, c * LANES:(c + 1) * LANES]
            u = gu[:, c * 2 * LANES + LANES:(c + 1) * 2 * LANES] + bl_ref[0, 0, :, c * LANES:(c + 1) * LANES]
            g = jnp.minimum(g, SWIGLU_LIMIT)
            u = jnp.clip(u, -SWIGLU_LIMIT, SWIGLU_LIMIT)
            act = g * jax.nn.sigmoid(SWIGLU_ALPHA * g) * (u + 1.0)
            act_ref[:, c * LANES:(c + 1) * LANES] = act.astype(BF16)

    @pl.when(nv_ref[b] == 0)
    def _():
        act_ref[...] = jnp.zeros(act_ref.shape, act_ref.dtype)

    @pl.when(cv_ref[b] == 1)
    def _():
        ii = lax.broadcasted_iota(jnp.int32, (2 * LANES, 2 * LANES), 0)
        jj = lax.broadcasted_iota(jnp.int32, (2 * LANES, 2 * LANES), 1)
        src = jnp.where(jj < LANES, 2 * jj, 2 * (jj - LANES) + 1)
        perm = jnp.where(ii == src, 1.0, 0.0).astype(BF16)

        def body(r, carry):
            rows = pl.ds(pl.multiple_of(r * PREP_ROWS, PREP_ROWS), PREP_ROWS)
            for c in range(n_groups):
                cols = slice(c * 2 * LANES, (c + 1) * 2 * LANES)
                wb = w_ref[0, 0, rows, cols].astype(BF16)
                wsc_ref[rows, cols] = jnp.dot(wb, perm, preferred_element_type=F32).astype(BF16)
            return carry

        lax.fori_loop(0, d // PREP_ROWS, body, 0)


def _moe_down_kernel(be_ref, bn_ref, bf_ref, bs_ref, nv_ref, cv_ref, a_ref, w_ref, bd_ref, o_ref, wsc_ref):
    b = pl.program_id(1)
    f = w_ref.shape[2]

    @pl.when(nv_ref[b] > 0)
    def _():
        o_ref[...] = jnp.dot(a_ref[...], wsc_ref[...], preferred_element_type=F32) + bd_ref[0, 0]

    @pl.when(nv_ref[b] == 0)
    def _():
        o_ref[...] = jnp.zeros(o_ref.shape, o_ref.dtype)

    @pl.when(cv_ref[b] == 1)
    def _():
        def body(r, carry):
            rows = pl.ds(pl.multiple_of(r * PREP_ROWS, PREP_ROWS), PREP_ROWS)
            wsc_ref[rows, :] = w_ref[0, 0, rows, :].astype(BF16)
            return carry

        lax.fori_loop(0, f // PREP_ROWS, body, 0)


def _to_bf16_kernel(x_ref, o_ref):
    o_ref[...] = x_ref[...].astype(BF16)


def _to_bf16(x, tm=MM_TM):
    m, d = x.shape
    tm = min(tm, m)
    return pl.pallas_call(
        _to_bf16_kernel,
        grid=(pl.cdiv(m, tm),),
        in_specs=[pl.BlockSpec((tm, d), lambda i: (i, 0))],
        out_specs=pl.BlockSpec((tm, d), lambda i: (i, 0)),
        out_shape=jax.ShapeDtypeStruct((m, d), BF16),
        compiler_params=pltpu.CompilerParams(dimension_semantics=("arbitrary",),
                                             vmem_limit_bytes=VMEM_LIMIT_BYTES),
        name="to_bf16",
    )(x)


def _moe_experts(xs, sched, w_up, bg, bl, w_down, b_down, layer, tm, tf, tn):
    rows, d = xs.shape
    nb = rows // tm
    f_dim = w_down.shape[2]
    nf = f_dim // tf
    nn = d // tn
    params = pltpu.CompilerParams(dimension_semantics=("arbitrary", "arbitrary"),
                                  vmem_limit_bytes=VMEM_LIMIT_BYTES)
    up_spec = pltpu.PrefetchScalarGridSpec(
        num_scalar_prefetch=6,
        grid=(nf, nb),
        in_specs=[
            pl.BlockSpec((tm, d), lambda f, b, be, bn, bf, bs, nv, cv: (bs[b], 0)),
            pl.BlockSpec((1, 1, d, 2 * tf),
                         lambda f, b, be, bn, bf, bs, nv, cv: (layer, bn[b], 0, jnp.minimum(f + bf[b], nf - 1))),
            pl.BlockSpec((1, 1, 1, tf), lambda f, b, be, bn, bf, bs, nv, cv: (layer, be[b], 0, f)),
            pl.BlockSpec((1, 1, 1, tf), lambda f, b, be, bn, bf, bs, nv, cv: (layer, be[b], 0, f)),
        ],
        out_specs=pl.BlockSpec((tm, tf), lambda f, b, be, bn, bf, bs, nv, cv: (b, f)),
        scratch_shapes=[pltpu.VMEM((d, 2 * tf), BF16)],
    )
    act = pl.pallas_call(
        _moe_up_kernel, grid_spec=up_spec,
        out_shape=jax.ShapeDtypeStruct((rows, f_dim), BF16),
        compiler_params=params, name="moe_up",
    )(*sched, xs, w_up, bg, bl)
    down_spec = pltpu.PrefetchScalarGridSpec(
        num_scalar_prefetch=6,
        grid=(nn, nb),
        in_specs=[
            pl.BlockSpec((tm, f_dim), lambda n, b, be, bn, bf, bs, nv, cv: (bs[b], 0)),
            pl.BlockSpec((1, 1, f_dim, tn),
                         lambda n, b, be, bn, bf, bs, nv, cv: (layer, bn[b], 0, jnp.minimum(n + bf[b], nn - 1))),
            pl.BlockSpec((1, 1, 1, tn), lambda n, b, be, bn, bf, bs, nv, cv: (layer, be[b], 0, n)),
        ],
        out_specs=pl.BlockSpec((tm, tn), lambda n, b, be, bn, bf, bs, nv, cv: (b, n)),
        scratch_shapes=[pltpu.VMEM((f_dim, tn), BF16)],
    )
    return pl.pallas_call(
        _moe_down_kernel, grid_spec=down_spec,
        out_shape=jax.ShapeDtypeStruct((rows, d), F32),
        compiler_params=params, name="moe_down",
    )(*sched, act, w_down, b_down)


def _moe_ffn(tok, w_router, b_router, w_up, bg, bl, w_down, b_down, layer,
             tm=MOE_TM, tf=MOE_TF, tn=MOE_TN):
    n_tok = tok.shape[0]
    n_exp = w_router.shape[1]
    logits = _router(tok, w_router, b_router)
    top_logit, top_idx = lax.top_k(logits, TOP_K)
    gates = jax.nn.softmax(top_logit, axis=-1)
    n_assign = n_tok * TOP_K
    nb = -(-n_assign // tm) + n_exp
    flat_e = top_idx.reshape(-1).astype(jnp.int32)
    onehot = (flat_e[:, None] == jnp.arange(n_exp, dtype=jnp.int32)[None, :]).astype(jnp.int32)
    csum = jnp.cumsum(onehot, axis=0)
    counts = csum[-1]
    rank = jnp.take_along_axis(csum, flat_e[:, None], axis=1)[:, 0] - 1
    blocks_e = (counts + tm - 1) // tm
    blk_end = jnp.cumsum(blocks_e)
    blk_start = blk_end - blocks_e
    n_used = blk_end[-1]
    dest = (blk_start[flat_e] + 1) * tm + rank
    src_tok = jnp.arange(n_assign, dtype=jnp.int32) // TOP_K
    row_tok = jnp.zeros(((nb + 1) * tm,), jnp.int32).at[dest].set(src_tok)
    def expert_of(blocks):
        return jnp.minimum(jnp.sum((blk_end[None, :] <= blocks[:, None]).astype(jnp.int32), axis=1), n_exp - 1)

    ids = jnp.minimum(jnp.arange(nb, dtype=jnp.int32), n_used - 1)
    own = expert_of(ids)
    own_end = blk_end[own]
    is_last_expert = own_end >= n_used
    after = expert_of(jnp.minimum(own_end, n_used - 1))
    staged = jnp.where(is_last_expert, own[0], after)
    valid = jnp.arange(nb, dtype=jnp.int32) < n_used
    refill = valid & (ids == own_end - 1) & ~is_last_expert
    one = jnp.ones((1,), jnp.int32)
    zero = jnp.zeros((1,), jnp.int32)
    sched = (jnp.concatenate([own[:1], own]),
             jnp.concatenate([own[:1], staged]),
             jnp.concatenate([zero, is_last_expert.astype(jnp.int32)]),
             jnp.concatenate([one, ids + 1]),
             jnp.concatenate([zero, valid.astype(jnp.int32)]),
             jnp.concatenate([one, refill.astype(jnp.int32)]))
    xs = _to_bf16(tok[row_tok], tm)
    ys = _moe_experts(xs, sched, w_up, bg, bl, w_down, b_down, layer, tm, tf, tn)
    picked = ys[dest.reshape(n_tok, TOP_K).T.reshape(-1)].reshape(TOP_K, n_tok, -1)
    return jnp.sum(picked * gates.T[:, :, None], axis=0)


def _segsum(p, ones_blk, split):
    hi = p.astype(BF16)
    parts = [hi]
    if split:
        parts.append((p - hi.astype(F32)).astype(BF16))
    n_seg = p.shape[1] // (2 * LANES)
    stacked = jnp.concatenate([x[:, j * 2 * LANES:(j + 1) * 2 * LANES] for x in parts for j in range(n_seg)], axis=0)
    r = jnp.dot(stacked, ones_blk, preferred_element_type=F32)
    rows = p.shape[0] * n_seg
    out = r[:rows]
    if split:
        out = out + r[rows:]
    return jnp.concatenate([out[j * p.shape[0]:(j + 1) * p.shape[0]] for j in range(n_seg)], axis=1)


def _rwkv_kernel(kk_ref, z_ref, w_ref, b_ref, k_ref, vt_ref, s0_ref, y_ref, st_ref, s_scr):
    nb, tc, width = kk_ref.shape
    hd = s_scr.shape[1]

    @pl.when(pl.program_id(1) == 0)
    def _():
        s_scr[...] = s0_ref[...]

    ri = lax.broadcasted_iota(jnp.int32, (2 * LANES, 2 * LANES), 0)
    ci = lax.broadcasted_iota(jnp.int32, (2 * LANES, 2 * LANES), 1)
    ones_blk = jnp.where((ri // hd) == (ci // hd), 1.0, 0.0).astype(BF16)
    er = lax.broadcasted_iota(jnp.int32, (LANES, width), 0)
    ec = lax.broadcasted_iota(jnp.int32, (LANES, width), 1)
    n_heads = width // hd
    expand = jnp.where(jnp.where(er < 3 * n_heads, er % n_heads, -1) == (ec // hd), 1.0, 0.0).astype(BF16)
    dr = lax.broadcasted_iota(jnp.int32, (hd, width), 0)
    dc = lax.broadcasted_iota(jnp.int32, (hd, width), 1)
    diag = (dc % hd) == dr

    def step(t, carry):
        for b in range(nb):
            s = s_scr[b]
            sa = _segsum(s * kk_ref[b, pl.ds(t, 1), :], ones_blk, True)
            q = _segsum(s * z_ref[b, pl.ds(t, 1), :], ones_blk, True)
            ve = jnp.dot(vt_ref[b, t], expand, preferred_element_type=F32)
            s_scr[b] = s * w_ref[b, pl.ds(t, 1), :] - sa * b_ref[b, pl.ds(t, 1), :] + ve * k_ref[b, pl.ds(t, 1), :]
            yd = jnp.where(diag, q, 0.0)
            y_ref[b, t] = jnp.sum(yd.reshape(hd // SUBLANES, SUBLANES, width), axis=0)
        return carry

    lax.fori_loop(0, tc, step, 0)
    st_ref[...] = s_scr[...]


def _rwkv_scan(kk, z, w, bb, k, vt, s0, tc):
    nb, t, width = kk.shape
    hd = s0.shape[1]
    gb = RWKV_BB
    row = pl.BlockSpec((gb, tc, width), lambda bi, ti: (bi, ti, 0))
    state = pl.BlockSpec((gb, hd, width), lambda bi, ti: (bi, 0, 0))
    return pl.pallas_call(
        _rwkv_kernel,
        grid=(nb // gb, t // tc),
        in_specs=[row, row, row, row, row,
                  pl.BlockSpec((gb, tc, hd, LANES), lambda bi, ti: (bi, ti, 0, 0)),
                  state],
        out_specs=[pl.BlockSpec((gb, tc, SUBLANES, width), lambda bi, ti: (bi, ti, 0, 0)),
                   state],
        out_shape=[jax.ShapeDtypeStruct((nb, t, SUBLANES, width), F32),
                   jax.ShapeDtypeStruct((nb, hd, width), F32)],
        scratch_shapes=[pltpu.VMEM((gb, hd, width), F32)],
        compiler_params=pltpu.CompilerParams(dimension_semantics=("arbitrary", "arbitrary"),
                                             vmem_limit_bytes=VMEM_LIMIT_BYTES),
        name="rwkv_scan",
    )(kk, z, w, bb, k, vt, s0)


def _lru_kernel(a_ref, x_ref, h0_ref, hs_ref, h_scr):
    @pl.when(pl.program_id(0) == 0)
    def _():
        h_scr[...] = h0_ref[...]

    def step(t, h):
        h = a_ref[t] * h + x_ref[t]
        hs_ref[t] = h
        return h

    h_scr[...] = lax.fori_loop(0, a_ref.shape[0], step, h_scr[...], unroll=8)


def _lru_scan(a, x, h0, tc=LRU_TC):
    t, nb, width = a.shape
    blk = pl.BlockSpec((tc, nb, width), lambda i: (i, 0, 0))
    return pl.pallas_call(
        _lru_kernel,
        grid=(t // tc,),
        in_specs=[blk, blk, pl.BlockSpec((nb, width), lambda i: (0, 0))],
        out_specs=blk,
        out_shape=jax.ShapeDtypeStruct((t, nb, width), F32),
        scratch_shapes=[pltpu.VMEM((nb, width), F32)],
        compiler_params=pltpu.CompilerParams(dimension_semantics=("arbitrary",),
                                             vmem_limit_bytes=VMEM_LIMIT_BYTES),
        name="lru_scan",
    )(a, x, h0)


def _rotary(x, pos):
    half = x.shape[-1] // 2
    inv = ROPE_BASE ** (-jnp.arange(half, dtype=F32) / half)
    ang = pos.astype(F32)[:, None] * inv[None, :]
    cos = jnp.cos(ang)[None, :, None, :]
    sin = jnp.sin(ang)[None, :, None, :]
    x1 = x[..., :half]
    x2 = x[..., half:]
    return jnp.concatenate([x1 * cos - x2 * sin, x1 * sin + x2 * cos], axis=-1)


def _chunk_gating(u, v, ln_g, ln_b, ws, bs):
    bq, t, _ = u.shape
    l = min(t, CHUNK)
    n = t // l
    vn = _layer_norm(v) * ln_g + ln_b
    mask = jnp.tril(jnp.ones((l, l), dtype=bool))
    w = jnp.where(mask[None], ws[:, :l, :l], 0.0)
    vh = vn.reshape(bq, n, l, A_HEADS, A_HD)
    mixed = jnp.einsum('hts,bnshd->bnthd', w, vh) + bs[:, :l].T[None, None, :, :, None]
    return u * mixed.reshape(bq, t, MIX_W), vn[:, t - l:]


def _rg_lru(xb, gb, h0, conv_buf, conv_w, conv_b, w_r, b_r, w_i, b_i, lam):
    bq, t, _ = xb.shape
    xpad = jnp.concatenate([conv_buf, xb], axis=1)
    xc = conv_b + sum(xpad[:, j:j + t] * conv_w[j] for j in range(CONV_W))
    xg = xc.reshape(bq, t, B_BLOCKS, B_BS)
    r = jax.nn.sigmoid(jnp.einsum('btnc,ncd->btnd', xg, w_r).reshape(bq, t, MIX_W) + b_r)
    i = jax.nn.sigmoid(jnp.einsum('btnc,ncd->btnd', xg, w_i).reshape(bq, t, MIX_W) + b_i)
    log_a = -RG_C * r * jax.nn.softplus(-lam)
    a = jnp.exp(log_a)
    xin = jnp.sqrt(-jnp.expm1(2.0 * log_a)) * (i * xc)

    hs = jnp.moveaxis(_lru_scan(jnp.moveaxis(a, 1, 0), jnp.moveaxis(xin, 1, 0), h0, min(LRU_TC, t)), 0, 1)
    y = hs * jax.nn.gelu(gb, approximate=True)
    return y, hs[:, -1], xpad[:, -(CONV_W - 1):]


def _retention(q, k, v, g, pos, s0, chunk, gn_g, gn_b):
    bq, t, _ = q.shape
    qh = _rotary(q.reshape(bq, t, C_HEADS, C_DK), pos)
    kh = _rotary(k.reshape(bq, t, C_HEADS, C_DK), pos) * (C_DK ** -0.5)
    vh = v.reshape(bq, t, C_HEADS, C_DV)
    lg = jnp.log(1.0 - 2.0 ** (-5.0 - jnp.arange(C_HEADS, dtype=F32)))
    n = t // chunk
    idx = jnp.arange(chunk, dtype=F32)
    diff = idx[:, None] - idx[None, :]
    intra = jnp.where(diff[None] >= 0, jnp.exp(jnp.maximum(diff, 0.0)[None] * lg[:, None, None]), 0.0)
    q_decay = jnp.exp((idx[:, None] + 1.0) * lg[None, :])
    k_decay = jnp.exp((chunk - 1.0 - idx[:, None]) * lg[None, :])
    chunk_decay = jnp.exp(chunk * lg)

    def to_blocks(x):
        return jnp.moveaxis(x.reshape(bq, n, chunk, C_HEADS, x.shape[-1]), 1, 0)

    def step(s, blk):
        qb, kb, vb = blk
        sc = jnp.einsum('bihd,bjhd->bhij', qb, kb) * intra[None]
        y = jnp.einsum('bhij,bjhe->bihe', sc, vb) + jnp.einsum('bihd,bhde->bihe', qb, s) * q_decay[None, :, :, None]
        s = s * chunk_decay[None, :, None, None] + jnp.einsum('bjhd,bjhe->bhde', kb * k_decay[None, :, :, None], vb)
        return s, y

    st, ys = lax.scan(step, s0, (to_blocks(qh), to_blocks(kh), to_blocks(vh)))
    y = jnp.moveaxis(ys, 0, 1).reshape(bq, t, C_HEADS, C_DV)
    y = _layer_norm(y).reshape(bq, t, MIX_W) * gn_g + gn_b
    return jax.nn.silu(g) * y, st


def _split3(x):
    hi = x.astype(BF16)
    r1 = x - hi.astype(F32)
    mid = r1.astype(BF16)
    lo = (r1 - mid.astype(F32)).astype(BF16)
    return hi, mid, lo


def _rwkv7(f, shift_buf, s0, mu, w0, w2, a0, a2, g2, k_k, k_a, r_k, lnx_g, lnx_b):
    bq, t, _ = f.shape
    prev = jnp.concatenate([shift_buf[:, None, :], f[:, :-1]], axis=1)
    fs = f + mu * (prev - f)
    r, k, v, fw, fa, fg = _split_cols(fs, D_SIZES)
    log_w = -jnp.exp(-jax.nn.softplus(-(w0 + jnp.tanh(fw) @ w2)) - 0.5)
    a = jax.nn.sigmoid(a0 + fa @ a2)
    g = jax.nn.sigmoid(fg) @ g2

    def heads(x):
        return x.reshape(bq, t, D_HEADS, D_HD)

    kk = heads(k * k_k)
    kk = kk * lax.rsqrt(jnp.maximum(jnp.sum(kk * kk, axis=-1, keepdims=True), 1e-24))
    k_h = heads(k * (1.0 + (a - 1.0) * k_a))
    r_h, v_h, w_h, a_h = heads(r), heads(v), heads(log_w), heads(a)

    def flat(x):
        return x.reshape(bq, t, MIX_W)

    def head_sum(x):
        return jnp.broadcast_to(jnp.sum(x, axis=-1, keepdims=True), x.shape)

    decay = jnp.exp(w_h)
    b_h = kk * a_h
    z_h = decay * r_h - kk * head_sum(b_h * r_h)
    vt = jnp.concatenate(_split3(jnp.swapaxes(v_h, 2, 3)), axis=-1)
    vt = jnp.pad(vt, ((0, 0), (0, 0), (0, 0), (0, LANES - vt.shape[-1])))
    s0k = jnp.swapaxes(s0, 1, 2).reshape(bq, D_HD, MIX_W)
    y8, st = _rwkv_scan(flat(kk), flat(z_h), flat(decay), flat(b_h), flat(k_h), vt, s0k, min(RWKV_TC, t))
    y = jnp.sum(y8, axis=2).reshape(bq, t, D_HEADS, D_HD) + head_sum(k_h * r_h) * v_h
    st = jnp.swapaxes(st.reshape(bq, D_HD, D_HEADS, D_HD), 1, 2)
    y = _layer_norm(y, RWKV_LN_EPS).reshape(bq, t, MIX_W) * lnx_g + lnx_b
    bonus = jnp.sum(r_h * k_h * r_k, axis=-1, keepdims=True) * v_h
    y = y + bonus.reshape(bq, t, MIX_W)
    return y * g, st, f[:, -1]


def _mixers(proj, l, pos, ret_chunk, h_rg, conv_rg, s_ret, s_rwkv, shift_rwkv, p):
    a_u, a_v, b_x, b_g, c_q, c_k, c_v, c_g, d_f = _split_cols(proj, IN_SIZES)
    y_a, vr = _chunk_gating(a_u, a_v, p['a_ln_g'][l], p['a_ln_b'][l], p['a_ws'][l], p['a_bs'][l])
    y_b, hb, cb = _rg_lru(b_x, b_g, h_rg, conv_rg, p['b_conv_w'][l], p['b_conv_b'][l],
                          p['b_wr'][l], p['b_br'][l], p['b_wi'][l], p['b_bi'][l], p['b_lambda'][l])
    y_c, sc = _retention(c_q, c_k, c_v, c_g, pos, s_ret, ret_chunk, p['c_gn_g'][l], p['c_gn_b'][l])
    y_d, sd, shd = _rwkv7(d_f, shift_rwkv, s_rwkv, p['d_mu'][l], p['d_w0'][l], p['d_w2'][l],
                          p['d_a0'][l], p['d_a2'][l], p['d_g2'][l], p['d_kk'][l], p['d_ka'][l],
                          p['d_rk'][l], p['d_lnx_g'][l], p['d_lnx_b'][l])
    return jnp.concatenate([y_a, y_b, y_c, y_d], axis=-1), (vr, hb, cb, sc, sd, shd)


def kernel(x_prompt, x_sample, c_prompt, c_sample, state_rglru_h, state_rglru_conv, state_retention, state_rwkv, state_rwkv_shift, w_ada, b_ada, ln_post_g, ln_post_b, w_in, w_out, a_ln_g, a_ln_b, a_ws, a_bs, b_conv_w, b_conv_b, b_wr, b_br, b_wi, b_bi, b_lambda, c_gn_g, c_gn_b, d_mu, d_w0, d_w2, d_a0, d_a2, d_g2, d_kk, d_ka, d_rk, d_lnx_g, d_lnx_b, w_router, b_router, w_up, b_up, w_down, b_down):
    p = dict(a_ln_g=a_ln_g, a_ln_b=a_ln_b, a_ws=a_ws, a_bs=a_bs, b_conv_w=b_conv_w, b_conv_b=b_conv_b,
             b_wr=b_wr, b_br=b_br, b_wi=b_wi, b_bi=b_bi, b_lambda=b_lambda, c_gn_g=c_gn_g, c_gn_b=c_gn_b,
             d_mu=d_mu, d_w0=d_w0, d_w2=d_w2, d_a0=d_a0, d_a2=d_a2, d_g2=d_g2, d_kk=d_kk, d_ka=d_ka,
             d_rk=d_rk, d_lnx_g=d_lnx_g, d_lnx_b=d_lnx_b)
    bp, tp, d = x_prompt.shape
    bs, ts, _ = x_sample.shape
    np_tok = bp * tp
    ns_tok = bs * ts

    c_act = jax.nn.silu(jnp.concatenate([c_prompt, c_sample], axis=0)).astype(BF16)

    def modulation(l, i):
        mod = _mm(c_act, w_ada, (l, i)) + b_ada[l, i]
        shift, scale, gate = jnp.split(mod, 3, axis=-1)

        def rows(m):
            return jnp.concatenate([jnp.repeat(m[:bp], tp, axis=0), jnp.repeat(m[bp:], ts, axis=0)], axis=0)

        return rows(shift), rows(scale), rows(1.0 + gate)

    zeros = lambda s: jnp.zeros((bp,) + s.shape[1:], s.dtype)
    st_p = (zeros(state_rglru_h), zeros(state_rglru_conv), zeros(state_retention), zeros(state_rwkv),
            zeros(state_rwkv_shift))
    st_s = (state_rglru_h, state_rglru_conv, state_retention, state_rwkv, state_rwkv_shift)
    pos_p = jnp.arange(tp, dtype=jnp.int32)
    pos_s = PAST_LEN + jnp.arange(ts, dtype=jnp.int32)

    bg = b_up[:, :, 0::2].reshape(DEPTH, N_EXPERTS, 1, D_FF)
    bl = b_up[:, :, 1::2].reshape(DEPTH, N_EXPERTS, 1, D_FF)
    bd = b_down.reshape(DEPTH, N_EXPERTS, 1, d)

    x = jnp.concatenate([x_prompt.reshape(np_tok, d), x_sample.reshape(ns_tok, d)], axis=0)
    outs_p, outs_s = [], []
    for l in range(DEPTH):
        shift, scale, gate = modulation(l, 0)
        h = (_layer_norm(x) * (1.0 + scale) + shift).astype(BF16)
        proj_p, proj_s = _mm_split(h, w_in, (l,), np_tok)
        mix_p, new_p = _mixers(proj_p.reshape(bp, tp, -1), l, pos_p, min(CHUNK, tp),
                               *(s[:, l] for s in st_p), p)
        mix_s, new_s = _mixers(proj_s.reshape(bs, ts, -1), l, pos_s, ts,
                               *(s[:, l] for s in st_s), p)
        outs_p.append(new_p)
        outs_s.append(new_s)
        mix_in = jnp.concatenate([mix_p.reshape(np_tok, d), mix_s.reshape(ns_tok, d)], axis=0).astype(BF16)
        mix = _mm(mix_in, w_out, (l,))
        x = _layer_norm(DEEPNORM_ALPHA * x + gate * mix) * ln_post_g[l, 0] + ln_post_b[l, 0]

        shift, scale, gate = modulation(l, 1)
        h = _layer_norm(x) * (1.0 + scale) + shift
        ffn = _moe_ffn(h, w_router[l], b_router[l], w_up, bg, bl, w_down, bd, l)
        x = _layer_norm(DEEPNORM_ALPHA * x + gate * ffn) * ln_post_g[l, 1] + ln_post_b[l, 1]

    def stack(outs, j):
        return jnp.stack([o[j] for o in outs], axis=1)

    y_prompt = x[:np_tok].reshape(bp, tp, d)
    y_sample = x[np_tok:].reshape(bs, ts, d)
    return (y_prompt, y_sample) + tuple(stack(outs_p, j) for j in range(6)) + tuple(stack(outs_s, j) for j in range(6))
```
